```python
import math
import jax, jax.numpy as jnp
from jax import lax
import numpy as np

D_MODEL = 1024
BATCH = 8
SEQ = 2048
DEPTH = 4

N_MIXERS = 2
ATTN_Q_HEADS = 16
ATTN_KV_HEADS = 2
ATTN_HEAD_DIM = 64
WINDOW = 128
ATTN_BLOCK = 128
RET_HEADS = 4
RET_QK_DIM = D_MODEL // RET_HEADS
RET_V_DIM = 2 * D_MODEL // RET_HEADS
RET_CHUNK = 128
ROPE_BASE = 10000.0
FFN_DIM = 2816
DEEPNORM_ALPHA = (2.0 * DEPTH) ** 0.25
DEEPNORM_BETA = (8.0 * DEPTH) ** -0.25
LN_EPS = 1e-5
GN_EPS = 1e-6
NEG_INF = -1e30

N_ATTN_LAYERS = (DEPTH + 1) // 2
N_RET_LAYERS = DEPTH // 2
ATTN_QKV_DIM = (ATTN_Q_HEADS + 2 * ATTN_KV_HEADS) * ATTN_HEAD_DIM
RET_PROJ_DIM = 2 * RET_HEADS * RET_QK_DIM + 2 * RET_HEADS * RET_V_DIM

kernel_name = "hybrid_swa_sink_retention_macaron_deepnorm"


def layer_norm(x, g, b):
    xf = x.astype(jnp.float32)
    mu = jnp.mean(xf, axis=-1, keepdims=True)
    var = jnp.mean(jnp.square(xf - mu), axis=-1, keepdims=True)
    y = (xf - mu) * lax.rsqrt(var + LN_EPS)
    return (y * g.astype(jnp.float32) + b.astype(jnp.float32)).astype(x.dtype)


def swiglu(x, w_gate_up, w_down):
    gate, up = jnp.split(x @ w_gate_up, 2, axis=-1)
    return (jax.nn.silu(gate) * up) @ w_down


def sliding_window_gqa(x, w_qkv, b_qkv, sinks, w_o, b_o):
    B, S, _ = x.shape
    H, KV, d, BLK = ATTN_Q_HEADS, ATTN_KV_HEADS, ATTN_HEAD_DIM, ATTN_BLOCK
    G = H // KV
    nb = S // BLK
    qkv = x @ w_qkv + b_qkv
    q = qkv[..., : H * d].reshape(B, nb, BLK, KV, G, d)
    k = qkv[..., H * d:(H + KV) * d].reshape(B, S, KV, d)
    v = qkv[..., (H + KV) * d:].reshape(B, S, KV, d)

    def band(t):
        prev = jnp.pad(t, ((0, 0), (BLK, 0), (0, 0), (0, 0)))[:, :S].reshape(B, nb, BLK, KV, d)
        cur = t.reshape(B, nb, BLK, KV, d)
        return jnp.concatenate([prev, cur], axis=2)

    kb, vb = band(k), band(v)
    scores = jnp.einsum('bnqkgd,bnjkd->bnkgqj', q, kb).astype(jnp.float32) * (d ** -0.5)
    qi = jnp.arange(BLK)[None, :, None]
    kj = jnp.arange(2 * BLK)[None, None, :]
    blk = jnp.arange(nb)[:, None, None]
    key_pos = blk * BLK - BLK + kj
    mask = (kj > qi + BLK - WINDOW) & (kj <= qi + BLK) & (key_pos >= 0)
    scores = jnp.where(mask[None, :, None, None], scores, NEG_INF)
    sink = jnp.broadcast_to(sinks.astype(jnp.float32).reshape(KV, G)[None, None, :, :, None, None],
                            scores.shape[:-1] + (1,))
    probs = jax.nn.softmax(jnp.concatenate([scores, sink], axis=-1), axis=-1)[..., :-1]
    o = jnp.einsum('bnkgqj,bnjkd->bnqkgd', probs.astype(x.dtype), vb).reshape(B, S, H * d)
    return o @ w_o + b_o


def rotary(t, pos):
    half = t.shape[-1] // 2
    freqs = 1.0 / (ROPE_BASE ** jnp.linspace(0.0, 1.0, half, dtype=jnp.float32))
    ang = pos[:, None] * freqs[None, :]
    cos = jnp.cos(ang)[None, :, None, :].astype(t.dtype)
    sin = jnp.sin(ang)[None, :, None, :].astype(t.dtype)
    t1, t2 = t[..., :half], t[..., half:]
    return jnp.concatenate([t1 * cos - t2 * sin, t1 * sin + t2 * cos], axis=-1)


def retention(x, w_qkvg, w_o):
    B, S, _ = x.shape
    H, dk, dv, C = RET_HEADS, RET_QK_DIM, RET_V_DIM, RET_CHUNK
    nc = S // C
    proj = x @ w_qkvg
    q = proj[..., : H * dk].reshape(B, S, H, dk)
    k = proj[..., H * dk: 2 * H * dk].reshape(B, S, H, dk)
    v = proj[..., 2 * H * dk: 2 * H * dk + H * dv].reshape(B, S, H, dv)
    g = proj[..., 2 * H * dk + H * dv:]
    pos = jnp.arange(S, dtype=jnp.float32)
    q = rotary(q, pos)
    k = rotary(k, pos) * (dk ** -0.5)

    log_gamma = jnp.log(1.0 - jnp.exp2(-5.0 - jnp.arange(H, dtype=jnp.float32)))
    idx = jnp.arange(C, dtype=jnp.float32)
    diff = idx[:, None] - idx[None, :]
    decay_in = jnp.where(diff >= 0, jnp.exp(jnp.maximum(diff, 0.0)[None] * log_gamma[:, None, None]), 0.0)
    q_decay = jnp.exp((idx + 1.0)[None, :] * log_gamma[:, None])
    k_decay = jnp.exp((C - 1.0 - idx)[None, :] * log_gamma[:, None])
    chunk_decay = jnp.exp(C * log_gamma).astype(x.dtype)

    qc = q.reshape(B, nc, C, H, dk)
    kc = k.reshape(B, nc, C, H, dk)
    vc = v.reshape(B, nc, C, H, dv)
    qk = jnp.einsum('bcnhd,bcmhd->bchnm', qc, kc) * decay_in.astype(x.dtype)[None, None]
    inner = jnp.einsum('bchnm,bcmhe->bcnhe', qk, vc)
    kd = kc * jnp.transpose(k_decay).astype(x.dtype)[None, None, :, :, None]
    kv = jnp.einsum('bcmhd,bcmhe->bchde', kd, vc)

    def step(state, kv_c):
        return state * chunk_decay[None, :, None, None] + kv_c, state

    _, prev = lax.scan(step, jnp.zeros((B, H, dk, dv), kv.dtype), jnp.moveaxis(kv, 1, 0))
    prev = jnp.moveaxis(prev, 0, 1)
    qd = qc * jnp.transpose(q_decay).astype(x.dtype)[None, None, :, :, None]
    cross = jnp.einsum('bcnhd,bchde->bcnhe', qd, prev)
    y = (inner + cross).reshape(B, S, H, dv)
    yf = y.astype(jnp.float32)
    mu = jnp.mean(yf, axis=-1, keepdims=True)
    var = jnp.mean(jnp.square(yf - mu), axis=-1, keepdims=True)
    y = ((yf - mu) * lax.rsqrt(var + GN_EPS)).astype(x.dtype).reshape(B, S, H * dv)
    return (jax.nn.silu(g) * y) @ w_o


def setup_inputs(seed: int = 0) -> dict:
    key = jax.random.key(seed)
    ks = jax.random.split(key, 14)
    f32 = jnp.float32
    D, F = D_MODEL, FFN_DIM
    nrm = lambda k, shape, s: jax.random.normal(k, shape, f32) * s
    return {
        "x": nrm(ks[0], (BATCH, SEQ, D), 1.0),
        "ln_g": 1.0 + nrm(ks[1], (DEPTH, 3, D), 0.02),
        "ln_b": nrm(ks[2], (DEPTH, 3, D), 0.02),
        "ffn1_w_gate_up": nrm(ks[3], (DEPTH, D, 2 * F), D ** -0.5),
        "ffn1_w_down": nrm(ks[4], (DEPTH, F, D), F ** -0.5 * DEEPNORM_BETA),
        "ffn2_w_gate_up": nrm(ks[5], (DEPTH, D, 2 * F), D ** -0.5),
        "ffn2_w_down": nrm(ks[6], (DEPTH, F, D), F ** -0.5 * DEEPNORM_BETA),
        "attn_w_qkv": nrm(ks[7], (N_ATTN_LAYERS, D, ATTN_QKV_DIM), D ** -0.5),
        "attn_b_qkv": nrm(ks[8], (N_ATTN_LAYERS, ATTN_QKV_DIM), 0.02),
        "attn_sinks": nrm(ks[9], (N_ATTN_LAYERS, ATTN_Q_HEADS), 1.0),
        "attn_w_o": nrm(ks[10], (N_ATTN_LAYERS, ATTN_Q_HEADS * ATTN_HEAD_DIM, D),
                        (ATTN_Q_HEADS * ATTN_HEAD_DIM) ** -0.5 * DEEPNORM_BETA),
        "attn_b_o": nrm(ks[11], (N_ATTN_LAYERS, D), 0.02),
        "ret_w_qkvg": nrm(ks[12], (N_RET_LAYERS, D, RET_PROJ_DIM), D ** -0.5),
        "ret_w_o": nrm(ks[13], (N_RET_LAYERS, RET_HEADS * RET_V_DIM, D),
                       (RET_HEADS * RET_V_DIM) ** -0.5 * DEEPNORM_BETA),
    }


def reference(x, ln_g, ln_b, ffn1_w_gate_up, ffn1_w_down, ffn2_w_gate_up, ffn2_w_down,
              attn_w_qkv, attn_b_qkv, attn_sinks, attn_w_o, attn_b_o, ret_w_qkvg, ret_w_o):
    a = DEEPNORM_ALPHA
    for i in range(DEPTH):
        x = layer_norm(a * x + 0.5 * swiglu(x, ffn1_w_gate_up[i], ffn1_w_down[i]), ln_g[i, 0], ln_b[i, 0])
        j = i // N_MIXERS
        if i % N_MIXERS == 0:
            mix = sliding_window_gqa(x, attn_w_qkv[j], attn_b_qkv[j], attn_sinks[j], attn_w_o[j], attn_b_o[j])
        else:
            mix = retention(x, ret_w_qkvg[j], ret_w_o[j])
        x = layer_norm(a * x + mix, ln_g[i, 1], ln_b[i, 1])
        x = layer_norm(a * x + 0.5 * swiglu(x, ffn2_w_gate_up[i], ffn2_w_down[i]), ln_g[i, 2], ln_b[i, 2])
    return x
```

```python
import functools
import math

import jax
import jax.numpy as jnp
from jax import lax
from jax.experimental import pallas as pl
from jax.experimental.pallas import tpu as pltpu

D_MODEL = 1024
DEPTH = 4
N_MIXERS = 2
ATTN_Q_HEADS = 16
ATTN_KV_HEADS = 2
ATTN_HEAD_DIM = 64
WINDOW = 128
ATTN_BLOCK = 128
RET_HEADS = 4
RET_QK_DIM = D_MODEL // RET_HEADS
RET_V_DIM = 2 * D_MODEL // RET_HEADS
RET_CHUNK = 128
ROPE_BASE = 10000.0
FFN_DIM = 2816
DEEPNORM_ALPHA = (2.0 * DEPTH) ** 0.25
LN_EPS = 1e-5
GN_EPS = 1e-6
NEG_INF = -1e30

ATTN_GROUP = ATTN_Q_HEADS // ATTN_KV_HEADS
ATTN_Q_DIM = ATTN_Q_HEADS * ATTN_HEAD_DIM
ATTN_KV_DIM = ATTN_KV_HEADS * ATTN_HEAD_DIM
RET_QK_ALL = RET_HEADS * RET_QK_DIM
RET_V_ALL = RET_HEADS * RET_V_DIM
RET_PROJ_DIM = 2 * RET_QK_ALL + 2 * RET_V_ALL
ROPE_HALF = RET_QK_DIM // 2
RET_LOG_GAMMA = tuple(math.log(1.0 - 2.0 ** (-5.0 - h)) for h in range(RET_HEADS))

V7X_LANES = 128
V7X_MXU_DIM = 256
V7X_VMEM_BYTES = 64 * 1024 * 1024

TOKEN_TILE = 512
FFN_CHUNK = V7X_MXU_DIM
PROJ_CHUNK = 2 * V7X_MXU_DIM

F32 = jnp.float32
BF16 = jnp.bfloat16


def _vmem_limit(block_bytes, temp_bytes):
    want = block_bytes + temp_bytes
    return int(min(want, V7X_VMEM_BYTES - 8 * 1024 * 1024))


def _resident(shape):
    return pl.BlockSpec(shape, lambda *_: (0,) * len(shape), pipeline_mode=pl.Buffered(1))


def _layer_norm(y, g, b):
    mu = jnp.mean(y, axis=-1, keepdims=True)
    yc = y - mu
    var = jnp.mean(yc * yc, axis=-1, keepdims=True)
    return yc * lax.rsqrt(var + LN_EPS) * g + b


def _mm(a, b):
    return jnp.dot(a, b, preferred_element_type=F32)


def _ffn_ln_body(x_ref, wgu_ref, wd_ref, g_ref, b_ref, o_ref):
    x = x_ref[...]
    xb = x.astype(BF16)
    acc = None
    for c in range(FFN_DIM // FFN_CHUNK):
        lo = c * FFN_CHUNK
        gate = _mm(xb, wgu_ref[:, lo:lo + FFN_CHUNK])
        up = _mm(xb, wgu_ref[:, FFN_DIM + lo:FFN_DIM + lo + FFN_CHUNK])
        h = (gate * jax.nn.sigmoid(gate) * up).astype(BF16)
        part = _mm(h, wd_ref[lo:lo + FFN_CHUNK, :])
        acc = part if acc is None else acc + part
    y = DEEPNORM_ALPHA * x + 0.5 * acc
    o_ref[...] = _layer_norm(y, g_ref[...], b_ref[...])


def _ffn_ln(x, w_gate_up, w_down, g, b):
    t, d = x.shape
    tm = TOKEN_TILE
    block_bytes = 2 * 2 * tm * d * 4 + (w_gate_up.size + w_down.size) * 2
    temp_bytes = 6 * tm * d * 4
    return pl.pallas_call(
        _ffn_ln_body,
        grid=(t // tm,),
        in_specs=[
            pl.BlockSpec((tm, d), lambda i: (i, 0)),
            _resident(w_gate_up.shape),
            _resident(w_down.shape),
            _resident((1, d)),
            _resident((1, d)),
        ],
        out_specs=pl.BlockSpec((tm, d), lambda i: (i, 0)),
        out_shape=jax.ShapeDtypeStruct((t, d), F32),
        compiler_params=pltpu.CompilerParams(
            dimension_semantics=("parallel",),
            vmem_limit_bytes=_vmem_limit(block_bytes, temp_bytes)),
        name="ffn_ln",
    )(x, w_gate_up, w_down, g.reshape(1, d), b.reshape(1, d))


def _proj_body(x_ref, w_ref, b_ref, o_ref):
    xb = x_ref[...].astype(BF16)
    n = o_ref.shape[-1]
    for lo in range(0, n, PROJ_CHUNK):
        hi = min(lo + PROJ_CHUNK, n)
        o_ref[:, lo:hi] = (_mm(xb, w_ref[:, lo:hi]) + b_ref[:, lo:hi]).astype(o_ref.dtype)


def _proj(x, w, bias):
    t, d = x.shape
    n = w.shape[1]
    tm = TOKEN_TILE
    block_bytes = 2 * tm * d * 4 + 2 * tm * n * 2 + w.size * 2 + n * 4
    temp_bytes = tm * d * 2 + 4 * tm * PROJ_CHUNK * 4
    return pl.pallas_call(
        _proj_body,
        grid=(t // tm,),
        in_specs=[
            pl.BlockSpec((tm, d), lambda i: (i, 0)),
            _resident(w.shape),
            _resident((1, n)),
        ],
        out_specs=pl.BlockSpec((tm, n), lambda i: (i, 0)),
        out_shape=jax.ShapeDtypeStruct((t, n), BF16),
        compiler_params=pltpu.CompilerParams(
            dimension_semantics=("parallel",),
            vmem_limit_bytes=_vmem_limit(block_bytes, temp_bytes)),
        name="proj",
    )(x, w, bias.reshape(1, n))


def _out_ln_body(y_ref, w_ref, bias_ref, x_ref, g_ref, b_ref, o_ref):
    mix = _mm(y_ref[...], w_ref[...]) + bias_ref[...]
    o_ref[...] = _layer_norm(DEEPNORM_ALPHA * x_ref[...] + mix, g_ref[...], b_ref[...])


def _out_ln(y, w, bias, x, g, b):
    t, k = y.shape
    d = w.shape[1]
    tm = TOKEN_TILE
    block_bytes = 2 * tm * k * 2 + 2 * 2 * tm * d * 4 + w.size * 2
    temp_bytes = 4 * tm * d * 4
    return pl.pallas_call(
        _out_ln_body,
        grid=(t // tm,),
        in_specs=[
            pl.BlockSpec((tm, k), lambda i: (i, 0)),
            _resident(w.shape),
            _resident((1, d)),
            pl.BlockSpec((tm, d), lambda i: (i, 0)),
            _resident((1, d)),
            _resident((1, d)),
        ],
        out_specs=pl.BlockSpec((tm, d), lambda i: (i, 0)),
        out_shape=jax.ShapeDtypeStruct((t, d), F32),
        compiler_params=pltpu.CompilerParams(
            dimension_semantics=("parallel",),
            vmem_limit_bytes=_vmem_limit(block_bytes, temp_bytes)),
        name="out_ln",
    )(y, w, bias.reshape(1, d), x, g.reshape(1, d), b.reshape(1, d))


ATTN_PAIR_LANES = 2 * ATTN_HEAD_DIM
ATTN_PAIRS_PER_KV = ATTN_GROUP // 2
ATTN_KV_DUP = ATTN_KV_HEADS * ATTN_PAIR_LANES
ATTN_PROJ_DIM = ATTN_Q_DIM + 2 * ATTN_KV_DUP
assert WINDOW == ATTN_BLOCK and ATTN_PAIR_LANES == V7X_LANES and ATTN_GROUP % 2 == 0


def _attn_body(sink_ref, q_ref, kv_ref, kvp_ref, o_ref):
    blk = ATTN_BLOCK
    first = pl.program_id(1) == 0
    lane = lax.broadcasted_iota(jnp.int32, (2 * blk, ATTN_PAIR_LANES), 1)
    low = lane < ATTN_HEAD_DIM
    qi = lax.broadcasted_iota(jnp.int32, (blk, blk), 0)
    kc = lax.broadcasted_iota(jnp.int32, (blk, blk), 1)
    from_prev = kc > qi + (blk - WINDOW)
    lane_q = lax.broadcasted_iota(jnp.int32, (blk, ATTN_PAIR_LANES), 1) < ATTN_HEAD_DIM
    zero = jnp.zeros((), BF16)

    for j in range(ATTN_KV_HEADS):
        kk = jnp.concatenate([kvp_ref[:, j * ATTN_PAIR_LANES:(j + 1) * ATTN_PAIR_LANES],
                              kv_ref[:, j * ATTN_PAIR_LANES:(j + 1) * ATTN_PAIR_LANES]], axis=0)
        vo = ATTN_KV_DUP + j * ATTN_PAIR_LANES
        vv = jnp.concatenate([kvp_ref[:, vo:vo + ATTN_PAIR_LANES], kv_ref[:, vo:vo + ATTN_PAIR_LANES]], axis=0)
        k_bd = jnp.concatenate([jnp.where(low, kk, zero), jnp.where(low, zero, kk)], axis=0)
        v_bd = jnp.concatenate([jnp.where(low, vv, zero), jnp.where(low, zero, vv)], axis=0)
        qo = j * ATTN_GROUP * ATTN_HEAD_DIM
        q_st = jnp.concatenate(
            [q_ref[:, qo + p * ATTN_PAIR_LANES:qo + (p + 1) * ATTN_PAIR_LANES] for p in range(ATTN_PAIRS_PER_KV)],
            axis=0)
        s = lax.dot_general(q_st, k_bd, (((1,), (1,)), ((), ())), preferred_element_type=F32)
        s = s * (ATTN_HEAD_DIM ** -0.5)
        p_rows = []
        inv = []
        for p in range(ATTN_PAIRS_PER_KV):
            tiles = []
            for par in range(2):
                head = j * ATTN_GROUP + 2 * p + par
                sink = sink_ref[head]
                base = par * 2 * blk
                s_prev = s[p * blk:(p + 1) * blk, base:base + blk]
                s_cur = s[p * blk:(p + 1) * blk, base + blk:base + 2 * blk]
                s_prev = jnp.where(first, NEG_INF, s_prev)
                logit = jnp.where(from_prev, s_prev, s_cur)
                m = jnp.maximum(jnp.max(logit, axis=-1, keepdims=True), sink)
                e = jnp.exp(logit - m)
                denom = jnp.sum(e, axis=-1, keepdims=True) + jnp.exp(sink - m)
                inv.append(1.0 / denom)
                eb = e.astype(BF16)
                tiles += [jnp.where(from_prev, eb, zero), jnp.where(from_prev, zero, eb)]
            p_rows.append(jnp.concatenate(tiles, axis=1))
        p_all = jnp.concatenate(p_rows, axis=0)
        o = _mm(p_all, v_bd)
        for p in range(ATTN_PAIRS_PER_KV):
            scale = jnp.where(lane_q, inv[2 * p], inv[2 * p + 1])
            o_ref[:, qo + p * ATTN_PAIR_LANES:qo + (p + 1) * ATTN_PAIR_LANES] = (
                o[p * blk:(p + 1) * blk, :] * scale).astype(o_ref.dtype)


def _attn_core(qkv, sinks, batch, seq):
    nb = seq // ATTN_BLOCK
    qkv = qkv.reshape(batch, seq, ATTN_PROJ_DIM)
    kv_w = 2 * ATTN_KV_DUP
    kv_col = ATTN_Q_DIM // kv_w
    block_bytes = 2 * ATTN_BLOCK * (2 * ATTN_Q_DIM + 2 * kv_w) * 2
    temp_bytes = 24 * ATTN_BLOCK * ATTN_Q_DIM * 4
    out = pl.pallas_call(
        _attn_body,
        grid=(batch, nb),
        in_specs=[
            pl.BlockSpec(memory_space=pltpu.SMEM),
            pl.BlockSpec((None, ATTN_BLOCK, ATTN_Q_DIM), lambda b, n: (b, n, 0)),
            pl.BlockSpec((None, ATTN_BLOCK, kv_w), lambda b, n: (b, n, kv_col)),
            pl.BlockSpec((None, ATTN_BLOCK, kv_w), lambda b, n: (b, jnp.maximum(n - 1, 0), kv_col)),
        ],
        out_specs=pl.BlockSpec((None, ATTN_BLOCK, ATTN_Q_DIM), lambda b, n: (b, n, 0)),
        out_shape=jax.ShapeDtypeStruct((batch, seq, ATTN_Q_DIM), BF16),
        compiler_params=pltpu.CompilerParams(
            dimension_semantics=("parallel", "parallel"),
            vmem_limit_bytes=_vmem_limit(block_bytes, temp_bytes)),
        name="swa_core",
    )(sinks, qkv, qkv, qkv)
    return out.reshape(batch * seq, ATTN_Q_DIM)


def _attn_qkv_layout(w_qkv, b_qkv):
    def dup(t, off):
        parts = []
        for j in range(ATTN_KV_HEADS):
            col = t[..., off + j * ATTN_HEAD_DIM:off + (j + 1) * ATTN_HEAD_DIM]
            parts += [col, col]
        return parts
    def relayout(t):
        return jnp.concatenate([t[..., :ATTN_Q_DIM]] + dup(t, ATTN_Q_DIM) + dup(t, ATTN_Q_DIM + ATTN_KV_DIM), axis=-1)
    return relayout(w_qkv), relayout(b_qkv)


def _ret_body(q_ref, k_ref, v_ref, g_ref, cos_ref, sin_ref, o_ref, state_ref):
    c = RET_CHUNK

    @pl.when(pl.program_id(1) == 0)
    def _():
        state_ref[...] = jnp.zeros_like(state_ref)

    cos = cos_ref[...]
    sin = sin_ref[...]
    row = lax.broadcasted_iota(jnp.int32, (c, c), 0).astype(F32)
    col = lax.broadcasted_iota(jnp.int32, (c, c), 1).astype(F32)
    diff = row - col
    idx = lax.broadcasted_iota(jnp.int32, (c, 1), 0).astype(F32)

    def rot(t):
        t1 = t[:, :ROPE_HALF].astype(F32)
        t2 = t[:, ROPE_HALF:].astype(F32)
        return jnp.concatenate([t1 * cos - t2 * sin, t1 * sin + t2 * cos], axis=-1)

    for h in range(RET_HEADS):
        lg = RET_LOG_GAMMA[h]
        q = rot(q_ref[:, h * RET_QK_DIM:(h + 1) * RET_QK_DIM])
        k = rot(k_ref[:, h * RET_QK_DIM:(h + 1) * RET_QK_DIM]) * (RET_QK_DIM ** -0.5)
        v = v_ref[:, h * RET_V_DIM:(h + 1) * RET_V_DIM]
        decay_in = jnp.where(diff >= 0, jnp.exp(jnp.maximum(diff, 0.0) * lg), 0.0)
        q_decay = jnp.exp((idx + 1.0) * lg)
        k_decay = jnp.exp((c - 1.0 - idx) * lg)
        chunk_decay = math.exp(c * lg)

        qb = q.astype(BF16)
        kb = k.astype(BF16)
        qk = lax.dot_general(qb, kb, (((1,), (1,)), ((), ())), preferred_element_type=F32) * decay_in
        inner = _mm(qk.astype(BF16), v)
        state = state_ref[h]
        cross = _mm((q * q_decay).astype(BF16), state.astype(BF16))
        kd = (k * k_decay).astype(BF16)
        kv = lax.dot_general(kd, v, (((0,), (0,)), ((), ())), preferred_element_type=F32)
        state_ref[h] = state * chunk_decay + kv

        y = inner + cross
        mu = jnp.mean(y, axis=-1, keepdims=True)
        yc = y - mu
        var = jnp.mean(yc * yc, axis=-1, keepdims=True)
        yn = yc * lax.rsqrt(var + GN_EPS)
        gate = g_ref[:, h * RET_V_DIM:(h + 1) * RET_V_DIM].astype(F32)
        o_ref[:, h * RET_V_DIM:(h + 1) * RET_V_DIM] = (gate * jax.nn.sigmoid(gate) * yn).astype(o_ref.dtype)


def _ret_core(proj, cos, sin, batch, seq):
    nc = seq // RET_CHUNK
    proj = proj.reshape(batch, seq, RET_PROJ_DIM)
    c = RET_CHUNK
    v_col = 2 * RET_QK_ALL // RET_V_ALL
    block_bytes = 2 * c * (2 * RET_QK_ALL + 3 * RET_V_ALL) * 2 + 4 * c * ROPE_HALF * 4
    temp_bytes = RET_HEADS * RET_QK_DIM * RET_V_DIM * 4 + 16 * c * RET_V_ALL * 4
    out = pl.pallas_call(
        _ret_body,
        grid=(batch, nc),
        in_specs=[
            pl.BlockSpec((None, c, RET_QK_ALL), lambda b, n: (b, n, 0)),
            pl.BlockSpec((None, c, RET_QK_ALL), lambda b, n: (b, n, 1)),
            pl.BlockSpec((None, c, RET_V_ALL), lambda b, n: (b, n, v_col)),
            pl.BlockSpec((None, c, RET_V_ALL), lambda b, n: (b, n, v_col + 1)),
            pl.BlockSpec((c, ROPE_HALF), lambda b, n: (n, 0)),
            pl.BlockSpec((c, ROPE_HALF), lambda b, n: (n, 0)),
        ],
        out_specs=pl.BlockSpec((None, c, RET_V_ALL), lambda b, n: (b, n, 0)),
        out_shape=jax.ShapeDtypeStruct((batch, seq, RET_V_ALL), BF16),
        scratch_shapes=[pltpu.VMEM((RET_HEADS, RET_QK_DIM, RET_V_DIM), F32)],
        compiler_params=pltpu.CompilerParams(
            dimension_semantics=("parallel", "arbitrary"),
            vmem_limit_bytes=_vmem_limit(block_bytes, temp_bytes)),
        name="retention_core",
    )(proj, proj, proj, proj, cos, sin)
    return out.reshape(batch * seq, RET_V_ALL)


def _rope_tables(seq):
    pos = jnp.arange(seq, dtype=F32)
    freqs = 1.0 / (ROPE_BASE ** jnp.linspace(0.0, 1.0, ROPE_HALF, dtype=F32))
    ang = pos[:, None] * freqs[None, :]
    return jnp.cos(ang), jnp.sin(ang)


def kernel(x, ln_g, ln_b, ffn1_w_gate_up, ffn1_w_down, ffn2_w_gate_up, ffn2_w_down,
           attn_w_qkv, attn_b_qkv, attn_sinks, attn_w_o, attn_b_o, ret_w_qkvg, ret_w_o):
    batch, seq, d = x.shape
    assert d == D_MODEL and seq % ATTN_BLOCK == 0 and seq % RET_CHUNK == 0 and (batch * seq) % TOKEN_TILE == 0
    cos, sin = _rope_tables(seq)
    zero_proj_bias = jnp.zeros((RET_PROJ_DIM,), F32)
    zero_out_bias = jnp.zeros((D_MODEL,), F32)
    h = x.reshape(batch * seq, d)
    for i in range(DEPTH):
        h = _ffn_ln(h, ffn1_w_gate_up[i].astype(BF16), ffn1_w_down[i].astype(BF16), ln_g[i, 0], ln_b[i, 0])
        j = i // N_MIXERS
        if i % N_MIXERS == 0:
            w_qkv, b_qkv = _attn_qkv_layout(attn_w_qkv[j], attn_b_qkv[j])
            qkv = _proj(h, w_qkv.astype(BF16), b_qkv)
            y = _attn_core(qkv, attn_sinks[j], batch, seq)
            h = _out_ln(y, attn_w_o[j].astype(BF16), attn_b_o[j], h, ln_g[i, 1], ln_b[i, 1])
        else:
            proj = _proj(h, ret_w_qkvg[j].astype(BF16), zero_proj_bias)
            y = _ret_core(proj, cos, sin, batch, seq)
            h = _out_ln(y, ret_w_o[j].astype(BF16), zero_out_bias, h, ln_g[i, 1], ln_b[i, 1])
        h = _ffn_ln(h, ffn2_w_gate_up[i].astype(BF16), ffn2_w_down[i].astype(BF16), ln_g[i, 2], ln_b[i, 2])
    return h.reshape(batch, seq, d)
```

```python
import functools
import math

import jax
import jax.numpy as jnp
from jax import lax
from jax.experimental import pallas as pl
from jax.experimental.pallas import tpu as pltpu

D_MODEL = 1024
DEPTH = 4
N_MIXERS = 2
ATTN_Q_HEADS = 16
ATTN_KV_HEADS = 2
ATTN_HEAD_DIM = 64
WINDOW = 128
ATTN_BLOCK = 128
RET_HEADS = 4
RET_QK_DIM = D_MODEL // RET_HEADS
RET_V_DIM = 2 * D_MODEL // RET_HEADS
RET_CHUNK = 128
ROPE_BASE = 10000.0
FFN_DIM = 2816
DEEPNORM_ALPHA = (2.0 * DEPTH) ** 0.25
LN_EPS = 1e-5
GN_EPS = 1e-6
NEG_INF = -1e30

ATTN_GROUP = ATTN_Q_HEADS // ATTN_KV_HEADS
ATTN_Q_DIM = ATTN_Q_HEADS * ATTN_HEAD_DIM
ATTN_KV_DIM = ATTN_KV_HEADS * ATTN_HEAD_DIM
RET_QK_ALL = RET_HEADS * RET_QK_DIM
RET_V_ALL = RET_HEADS * RET_V_DIM
RET_PROJ_DIM = 2 * RET_QK_ALL + 2 * RET_V_ALL
ROPE_HALF = RET_QK_DIM // 2
RET_LOG_GAMMA = tuple(math.log(1.0 - 2.0 ** (-5.0 - h)) for h in range(RET_HEADS))

V7X_LANES = 128
V7X_MXU_DIM = 256
V7X_VMEM_BYTES = 64 * 1024 * 1024

TOKEN_TILE = 512
FFN_CHUNK = V7X_MXU_DIM
PROJ_CHUNK = 2 * V7X_MXU_DIM

F32 = jnp.float32
BF16 = jnp.bfloat16


def _vmem_limit(block_bytes, temp_bytes):
    want = block_bytes + temp_bytes
    return int(min(want, V7X_VMEM_BYTES - 8 * 1024 * 1024))


def _resident(shape):
    return pl.BlockSpec(shape, lambda *_: (0,) * len(shape), pipeline_mode=pl.Buffered(1))


def _layer_norm(y, g, b):
    mu = jnp.mean(y, axis=-1, keepdims=True)
    yc = y - mu
    var = jnp.mean(yc * yc, axis=-1, keepdims=True)
    return yc * lax.rsqrt(var + LN_EPS) * g + b


def _mm(a, b):
    return jnp.dot(a, b, preferred_element_type=F32)


def _ffn_ln_body(x_ref, wgu_ref, wd_ref, g_ref, b_ref, o_ref):
    x = x_ref[...]
    xb = x.astype(BF16)
    acc = None
    for c in range(FFN_DIM // FFN_CHUNK):
        lo = c * FFN_CHUNK
        gate = _mm(xb, wgu_ref[:, lo:lo + FFN_CHUNK].astype(BF16))
        up = _mm(xb, wgu_ref[:, FFN_DIM + lo:FFN_DIM + lo + FFN_CHUNK].astype(BF16))
        h = (gate * jax.nn.sigmoid(gate) * up).astype(BF16)
        part = _mm(h, wd_ref[lo:lo + FFN_CHUNK, :].astype(BF16))
        acc = part if acc is None else acc + part
    y = DEEPNORM_ALPHA * x + 0.5 * acc
    o_ref[...] = _layer_norm(y, g_ref[...], b_ref[...])


def _layer_block(shape, layer):
    return pl.BlockSpec((None,) + tuple(shape[1:]), lambda *_: (layer,) + (0,) * (len(shape) - 1),
                        pipeline_mode=pl.Buffered(1))


def _ffn_ln(x, w_gate_up, w_down, layer, g, b):
    t, d = x.shape
    tm = TOKEN_TILE
    block_bytes = 2 * 2 * tm * d * 4 + (w_gate_up[0].size + w_down[0].size) * 4
    temp_bytes = 6 * tm * d * 4
    return pl.pallas_call(
        _ffn_ln_body,
        grid=(t // tm,),
        in_specs=[
            pl.BlockSpec((tm, d), lambda i: (i, 0)),
            _layer_block(w_gate_up.shape, layer),
            _layer_block(w_down.shape, layer),
            _resident((1, d)),
            _resident((1, d)),
        ],
        out_specs=pl.BlockSpec((tm, d), lambda i: (i, 0)),
        out_shape=jax.ShapeDtypeStruct((t, d), F32),
        compiler_params=pltpu.CompilerParams(
            dimension_semantics=("parallel",),
            vmem_limit_bytes=_vmem_limit(block_bytes, temp_bytes)),
        name="ffn_ln",
    )(x, w_gate_up, w_down, g.reshape(1, d), b.reshape(1, d))


def _proj_body(x_ref, w_ref, b_ref, o_ref):
    xb = x_ref[...].astype(BF16)
    n = o_ref.shape[-1]
    for lo in range(0, n, PROJ_CHUNK):
        hi = min(lo + PROJ_CHUNK, n)
        o_ref[:, lo:hi] = (_mm(xb, w_ref[:, lo:hi]) + b_ref[:, lo:hi]).astype(o_ref.dtype)


def _proj(x, w, layer, bias):
    t, d = x.shape
    n = w.shape[-1]
    tm = TOKEN_TILE
    block_bytes = 2 * tm * d * 4 + 2 * tm * n * 2 + w[0].size * 2 + n * 4
    temp_bytes = tm * d * 2 + 4 * tm * PROJ_CHUNK * 4
    return pl.pallas_call(
        _proj_body,
        grid=(t // tm,),
        in_specs=[
            pl.BlockSpec((tm, d), lambda i: (i, 0)),
            _layer_block(w.shape, layer),
            _resident((1, n)),
        ],
        out_specs=pl.BlockSpec((tm, n), lambda i: (i, 0)),
        out_shape=jax.ShapeDtypeStruct((t, n), BF16),
        compiler_params=pltpu.CompilerParams(
            dimension_semantics=("parallel",),
            vmem_limit_bytes=_vmem_limit(block_bytes, temp_bytes)),
        name="proj",
    )(x, w, bias.reshape(1, n))


def _out_ln_body(y_ref, w_ref, bias_ref, x_ref, g_ref, b_ref, o_ref):
    mix = _mm(y_ref[...], w_ref[...]) + bias_ref[...]
    o_ref[...] = _layer_norm(DEEPNORM_ALPHA * x_ref[...] + mix, g_ref[...], b_ref[...])


def _out_ln(y, w, layer, bias, x, g, b):
    t, k = y.shape
    d = w.shape[-1]
    tm = TOKEN_TILE
    block_bytes = 2 * tm * k * 2 + 2 * 2 * tm * d * 4 + w[0].size * 2
    temp_bytes = 4 * tm * d * 4
    return pl.pallas_call(
        _out_ln_body,
        grid=(t // tm,),
        in_specs=[
            pl.BlockSpec((tm, k), lambda i: (i, 0)),
            _layer_block(w.shape, layer),
            _resident((1, d)),
            pl.BlockSpec((tm, d), lambda i: (i, 0)),
            _resident((1, d)),
            _resident((1, d)),
        ],
        out_specs=pl.BlockSpec((tm, d), lambda i: (i, 0)),
        out_shape=jax.ShapeDtypeStruct((t, d), F32),
        compiler_params=pltpu.CompilerParams(
            dimension_semantics=("parallel",),
            vmem_limit_bytes=_vmem_limit(block_bytes, temp_bytes)),
        name="out_ln",
    )(y, w, bias.reshape(1, d), x, g.reshape(1, d), b.reshape(1, d))


ATTN_PAIR_LANES = 2 * ATTN_HEAD_DIM
ATTN_PAIRS_PER_KV = ATTN_GROUP // 2
ATTN_KV_DUP = ATTN_KV_HEADS * ATTN_PAIR_LANES
ATTN_PROJ_DIM = ATTN_Q_DIM + 2 * ATTN_KV_DUP
assert WINDOW == ATTN_BLOCK and ATTN_PAIR_LANES == V7X_LANES and ATTN_GROUP % 2 == 0


def _attn_body(sink_ref, q_ref, kv_ref, kvp_ref, o_ref):
    blk = ATTN_BLOCK
    first = pl.program_id(1) == 0
    lane = lax.broadcasted_iota(jnp.int32, (2 * blk, ATTN_PAIR_LANES), 1)
    low = lane < ATTN_HEAD_DIM
    qi = lax.broadcasted_iota(jnp.int32, (blk, blk), 0)
    kc = lax.broadcasted_iota(jnp.int32, (blk, blk), 1)
    from_prev = kc > qi + (blk - WINDOW)
    lane_q = lax.broadcasted_iota(jnp.int32, (blk, ATTN_PAIR_LANES), 1) < ATTN_HEAD_DIM
    zero = jnp.zeros((), BF16)

    for j in range(ATTN_KV_HEADS):
        kk = jnp.concatenate([kvp_ref[:, j * ATTN_PAIR_LANES:(j + 1) * ATTN_PAIR_LANES],
                              kv_ref[:, j * ATTN_PAIR_LANES:(j + 1) * ATTN_PAIR_LANES]], axis=0)
        vo = ATTN_KV_DUP + j * ATTN_PAIR_LANES
        vv = jnp.concatenate([kvp_ref[:, vo:vo + ATTN_PAIR_LANES], kv_ref[:, vo:vo + ATTN_PAIR_LANES]], axis=0)
        k_bd = jnp.concatenate([jnp.where(low, kk, zero), jnp.where(low, zero, kk)], axis=0)
        v_bd = jnp.concatenate([jnp.where(low, vv, zero), jnp.where(low, zero, vv)], axis=0)
        qo = j * ATTN_GROUP * ATTN_HEAD_DIM
        q_st = jnp.concatenate(
            [q_ref[:, qo + p * ATTN_PAIR_LANES:qo + (p + 1) * ATTN_PAIR_LANES] for p in range(ATTN_PAIRS_PER_KV)],
            axis=0)
        s = lax.dot_general(q_st, k_bd, (((1,), (1,)), ((), ())), preferred_element_type=F32)
        s = s * (ATTN_HEAD_DIM ** -0.5)
        p_rows = []
        inv = []
        for p in range(ATTN_PAIRS_PER_KV):
            tiles = []
            for par in range(2):
                head = j * ATTN_GROUP + 2 * p + par
                sink = sink_ref[head]
                base = par * 2 * blk
                s_prev = s[p * blk:(p + 1) * blk, base:base + blk]
                s_cur = s[p * blk:(p + 1) * blk, base + blk:base + 2 * blk]
                s_prev = jnp.where(first, NEG_INF, s_prev)
                logit = jnp.where(from_prev, s_prev, s_cur)
                m = jnp.maximum(jnp.max(logit, axis=-1, keepdims=True), sink)
                e = jnp.exp(logit - m)
                denom = jnp.sum(e, axis=-1, keepdims=True) + jnp.exp(sink - m)
                inv.append(1.0 / denom)
                eb = e.astype(BF16)
                tiles += [jnp.where(from_prev, eb, zero), jnp.where(from_prev, zero, eb)]
            p_rows.append(jnp.concatenate(tiles, axis=1))
        p_all = jnp.concatenate(p_rows, axis=0)
        o = _mm(p_all, v_bd)
        for p in range(ATTN_PAIRS_PER_KV):
            scale = jnp.where(lane_q, inv[2 * p], inv[2 * p + 1])
            o_ref[:, qo + p * ATTN_PAIR_LANES:qo + (p + 1) * ATTN_PAIR_LANES] = (
                o[p * blk:(p + 1) * blk, :] * scale).astype(o_ref.dtype)


def _attn_core(qkv, sinks, batch, seq):
    nb = seq // ATTN_BLOCK
    qkv = qkv.reshape(batch, seq, ATTN_PROJ_DIM)
    kv_w = 2 * ATTN_KV_DUP
    kv_col = ATTN_Q_DIM // kv_w
    block_bytes = 2 * ATTN_BLOCK * (2 * ATTN_Q_DIM + 2 * kv_w) * 2
    temp_bytes = 24 * ATTN_BLOCK * ATTN_Q_DIM * 4
    out = pl.pallas_call(
        _attn_body,
        grid=(batch, nb),
        in_specs=[
            pl.BlockSpec(memory_space=pltpu.SMEM),
            pl.BlockSpec((None, ATTN_BLOCK, ATTN_Q_DIM), lambda b, n: (b, n, 0)),
            pl.BlockSpec((None, ATTN_BLOCK, kv_w), lambda b, n: (b, n, kv_col)),
            pl.BlockSpec((None, ATTN_BLOCK, kv_w), lambda b, n: (b, jnp.maximum(n - 1, 0), kv_col)),
        ],
        out_specs=pl.BlockSpec((None, ATTN_BLOCK, ATTN_Q_DIM), lambda b, n: (b, n, 0)),
        out_shape=jax.ShapeDtypeStruct((batch, seq, ATTN_Q_DIM), BF16),
        compiler_params=pltpu.CompilerParams(
            dimension_semantics=("parallel", "parallel"),
            vmem_limit_bytes=_vmem_limit(block_bytes, temp_bytes)),
        name="swa_core",
    )(sinks, qkv, qkv, qkv)
    return out.reshape(batch * seq, ATTN_Q_DIM)


def _attn_qkv_layout(w_qkv, b_qkv):
    def dup(t, off):
        parts = []
        for j in range(ATTN_KV_HEADS):
            col = t[..., off + j * ATTN_HEAD_DIM:off + (j + 1) * ATTN_HEAD_DIM]
            parts += [col, col]
        return parts
    def relayout(t):
        return jnp.concatenate([t[..., :ATTN_Q_DIM]] + dup(t, ATTN_Q_DIM) + dup(t, ATTN_Q_DIM + ATTN_KV_DIM), axis=-1)
    return relayout(w_qkv), relayout(b_qkv)


def _ret_body(q_ref, k_ref, v_ref, g_ref, cos_ref, sin_ref, o_ref, state_ref):
    c = RET_CHUNK

    @pl.when(pl.program_id(1) == 0)
    def _():
        state_ref[...] = jnp.zeros_like(state_ref)

    cos = cos_ref[...]
    sin = sin_ref[...]
    row = lax.broadcasted_iota(jnp.int32, (c, c), 0).astype(F32)
    col = lax.broadcasted_iota(jnp.int32, (c, c), 1).astype(F32)
    diff = row - col
    idx = lax.broadcasted_iota(jnp.int32, (c, 1), 0).astype(F32)

    def rot(t):
        t1 = t[:, :ROPE_HALF].astype(F32)
        t2 = t[:, ROPE_HALF:].astype(F32)
        return jnp.concatenate([t1 * cos - t2 * sin, t1 * sin + t2 * cos], axis=-1)

    for h in range(RET_HEADS):
        lg = RET_LOG_GAMMA[h]
        q = rot(q_ref[:, h * RET_QK_DIM:(h + 1) * RET_QK_DIM])
        k = rot(k_ref[:, h * RET_QK_DIM:(h + 1) * RET_QK_DIM]) * (RET_QK_DIM ** -0.5)
        v = v_ref[:, h * RET_V_DIM:(h + 1) * RET_V_DIM]
        decay_in = jnp.where(diff >= 0, jnp.exp(jnp.maximum(diff, 0.0) * lg), 0.0)
        q_decay = jnp.exp((idx + 1.0) * lg)
        k_decay = jnp.exp((c - 1.0 - idx) * lg)
        chunk_decay = math.exp(c * lg)

        qb = q.astype(BF16)
        kb = k.astype(BF16)
        qk = lax.dot_general(qb, kb, (((1,), (1,)), ((), ())), preferred_element_type=F32) * decay_in
        inner = _mm(qk.astype(BF16), v)
        state = state_ref[h]
        cross = _mm((q * q_decay).astype(BF16), state.astype(BF16))
        kd = (k * k_decay).astype(BF16)
        kv = lax.dot_general(kd, v, (((0,), (0,)), ((), ())), preferred_element_type=F32)
        state_ref[h] = state * chunk_decay + kv

        y = inner + cross
        mu = jnp.mean(y, axis=-1, keepdims=True)
        yc = y - mu
        var = jnp.mean(yc * yc, axis=-1, keepdims=True)
        yn = yc * lax.rsqrt(var + GN_EPS)
        gate = g_ref[:, h * RET_V_DIM:(h + 1) * RET_V_DIM].astype(F32)
        o_ref[:, h * RET_V_DIM:(h + 1) * RET_V_DIM] = (gate * jax.nn.sigmoid(gate) * yn).astype(o_ref.dtype)


def _ret_core(proj, cos, sin, batch, seq):
    nc = seq // RET_CHUNK
    proj = proj.reshape(batch, seq, RET_PROJ_DIM)
    c = RET_CHUNK
    v_col = 2 * RET_QK_ALL // RET_V_ALL
    block_bytes = 2 * c * (2 * RET_QK_ALL + 3 * RET_V_ALL) * 2 + 4 * c * ROPE_HALF * 4
    temp_bytes = RET_HEADS * RET_QK_DIM * RET_V_DIM * 4 + 16 * c * RET_V_ALL * 4
    out = pl.pallas_call(
        _ret_body,
        grid=(batch, nc),
        in_specs=[
            pl.BlockSpec((None, c, RET_QK_ALL), lambda b, n: (b, n, 0)),
            pl.BlockSpec((None, c, RET_QK_ALL), lambda b, n: (b, n, 1)),
            pl.BlockSpec((None, c, RET_V_ALL), lambda b, n: (b, n, v_col)),
            pl.BlockSpec((None, c, RET_V_ALL), lambda b, n: (b, n, v_col + 1)),
            pl.BlockSpec((c, ROPE_HALF), lambda b, n: (n, 0)),
            pl.BlockSpec((c, ROPE_HALF), lambda b, n: (n, 0)),
        ],
        out_specs=pl.BlockSpec((None, c, RET_V_ALL), lambda b, n: (b, n, 0)),
        out_shape=jax.ShapeDtypeStruct((batch, seq, RET_V_ALL), BF16),
        scratch_shapes=[pltpu.VMEM((RET_HEADS, RET_QK_DIM, RET_V_DIM), F32)],
        compiler_params=pltpu.CompilerParams(
            dimension_semantics=("parallel", "arbitrary"),
            vmem_limit_bytes=_vmem_limit(block_bytes, temp_bytes)),
        name="retention_core",
    )(proj, proj, proj, proj, cos, sin)
    return out.reshape(batch * seq, RET_V_ALL)


def _rope_tables(seq):
    pos = jnp.arange(seq, dtype=F32)
    freqs = 1.0 / (ROPE_BASE ** jnp.linspace(0.0, 1.0, ROPE_HALF, dtype=F32))
    ang = pos[:, None] * freqs[None, :]
    return jnp.cos(ang), jnp.sin(ang)


def kernel(x, ln_g, ln_b, ffn1_w_gate_up, ffn1_w_down, ffn2_w_gate_up, ffn2_w_down,
           attn_w_qkv, attn_b_qkv, attn_sinks, attn_w_o, attn_b_o, ret_w_qkvg, ret_w_o):
    batch, seq, d = x.shape
    assert d == D_MODEL and seq % ATTN_BLOCK == 0 and seq % RET_CHUNK == 0 and (batch * seq) % TOKEN_TILE == 0
    cos, sin = _rope_tables(seq)
    zero_proj_bias = jnp.zeros((RET_PROJ_DIM,), F32)
    zero_out_bias = jnp.zeros((D_MODEL,), F32)
    w_qkv, b_qkv = _attn_qkv_layout(attn_w_qkv, attn_b_qkv)
    w_qkv = w_qkv.astype(BF16)
    attn_wo = attn_w_o.astype(BF16)
    ret_w = ret_w_qkvg.astype(BF16)
    ret_wo = ret_w_o.astype(BF16)
    h = x.reshape(batch * seq, d)
    for i in range(DEPTH):
        h = _ffn_ln(h, ffn1_w_gate_up, ffn1_w_down, i, ln_g[i, 0], ln_b[i, 0])
        j = i // N_MIXERS
        if i % N_MIXERS == 0:
            qkv = _proj(h, w_qkv, j, b_qkv[j])
            y = _attn_core(qkv, attn_sinks[j], batch, seq)
            h = _out_ln(y, attn_wo, j, attn_b_o[j], h, ln_g[i, 1], ln_b[i, 1])
        else:
            proj = _proj(h, ret_w, j, zero_proj_bias)
            y = _ret_core(proj, cos, sin, batch, seq)
            h = _out_ln(y, ret_wo, j, zero_out_bias, h, ln_g[i, 1], ln_b[i, 1])
        h = _ffn_ln(h, ffn2_w_gate_up, ffn2_w_down, i, ln_g[i, 2], ln_b[i, 2])
    return h.reshape(batch, seq, d)
```

```python
import functools
import math

import jax
import jax.numpy as jnp
from jax import lax
from jax.experimental import pallas as pl
from jax.experimental.pallas import tpu as pltpu

D_MODEL = 1024
DEPTH = 4
N_MIXERS = 2
ATTN_Q_HEADS = 16
ATTN_KV_HEADS = 2
ATTN_HEAD_DIM = 64
WINDOW = 128
ATTN_BLOCK = 128
RET_HEADS = 4
RET_QK_DIM = D_MODEL // RET_HEADS
RET_V_DIM = 2 * D_MODEL // RET_HEADS
RET_CHUNK = 128
ROPE_BASE = 10000.0
FFN_DIM = 2816
DEEPNORM_ALPHA = (2.0 * DEPTH) ** 0.25
LN_EPS = 1e-5
GN_EPS = 1e-6
NEG_INF = -1e30

ATTN_GROUP = ATTN_Q_HEADS // ATTN_KV_HEADS
ATTN_Q_DIM = ATTN_Q_HEADS * ATTN_HEAD_DIM
ATTN_KV_DIM = ATTN_KV_HEADS * ATTN_HEAD_DIM
RET_QK_ALL = RET_HEADS * RET_QK_DIM
RET_V_ALL = RET_HEADS * RET_V_DIM
RET_PROJ_DIM = 2 * RET_QK_ALL + 2 * RET_V_ALL
ROPE_HALF = RET_QK_DIM // 2
RET_LOG_GAMMA = tuple(math.log(1.0 - 2.0 ** (-5.0 - h)) for h in range(RET_HEADS))

V7X_LANES = 128
V7X_MXU_DIM = 256
V7X_VMEM_BYTES = 64 * 1024 * 1024

TOKEN_TILE = 512
FFN_CHUNK = V7X_MXU_DIM
PROJ_CHUNK = 2 * V7X_MXU_DIM

F32 = jnp.float32
BF16 = jnp.bfloat16


def _vmem_limit(block_bytes, temp_bytes):
    want = block_bytes + temp_bytes
    return int(min(want, V7X_VMEM_BYTES - 8 * 1024 * 1024))


def _resident(shape):
    return pl.BlockSpec(shape, lambda *_: (0,) * len(shape), pipeline_mode=pl.Buffered(1))


def _layer_norm(y, g, b):
    mu = jnp.mean(y, axis=-1, keepdims=True)
    yc = y - mu
    var = jnp.mean(yc * yc, axis=-1, keepdims=True)
    return yc * lax.rsqrt(var + LN_EPS) * g + b


def _mm(a, b):
    return jnp.dot(a, b, preferred_element_type=F32)


def _ffn_ln_body(x_ref, wgu_ref, wd_ref, g_ref, b_ref, o_ref):
    x = x_ref[...]
    xb = x.astype(BF16)
    acc = None
    for c in range(FFN_DIM // FFN_CHUNK):
        lo = c * FFN_CHUNK
        gate = _mm(xb, wgu_ref[:, lo:lo + FFN_CHUNK].astype(BF16))
        up = _mm(xb, wgu_ref[:, FFN_DIM + lo:FFN_DIM + lo + FFN_CHUNK].astype(BF16))
        h = (gate * jax.nn.sigmoid(gate) * up).astype(BF16)
        part = _mm(h, wd_ref[lo:lo + FFN_CHUNK, :].astype(BF16))
        acc = part if acc is None else acc + part
    y = DEEPNORM_ALPHA * x + 0.5 * acc
    o_ref[...] = _layer_norm(y, g_ref[...], b_ref[...])


def _layer_block(shape, layer):
    return pl.BlockSpec((None,) + tuple(shape[1:]), lambda *_: (layer,) + (0,) * (len(shape) - 1),
                        pipeline_mode=pl.Buffered(1))


def _ffn_ln(x, w_gate_up, w_down, layer, g, b):
    t, d = x.shape
    tm = TOKEN_TILE
    block_bytes = 2 * 2 * tm * d * 4 + (w_gate_up[0].size + w_down[0].size) * 4
    temp_bytes = 6 * tm * d * 4
    return pl.pallas_call(
        _ffn_ln_body,
        grid=(t // tm,),
        in_specs=[
            pl.BlockSpec((tm, d), lambda i: (i, 0)),
            _layer_block(w_gate_up.shape, layer),
            _layer_block(w_down.shape, layer),
            _resident((1, d)),
            _resident((1, d)),
        ],
        out_specs=pl.BlockSpec((tm, d), lambda i: (i, 0)),
        out_shape=jax.ShapeDtypeStruct((t, d), F32),
        compiler_params=pltpu.CompilerParams(
            dimension_semantics=("parallel",),
            vmem_limit_bytes=_vmem_limit(block_bytes, temp_bytes)),
        name="ffn_ln",
    )(x, w_gate_up, w_down, g.reshape(1, d), b.reshape(1, d))


def _proj_body(x_ref, w_ref, b_ref, o_ref):
    xb = x_ref[...].astype(BF16)
    n = o_ref.shape[-1]
    for lo in range(0, n, PROJ_CHUNK):
        hi = min(lo + PROJ_CHUNK, n)
        o_ref[:, lo:hi] = (_mm(xb, w_ref[:, lo:hi]) + b_ref[:, lo:hi]).astype(o_ref.dtype)


def _proj(x, w, layer, bias):
    t, d = x.shape
    n = w.shape[-1]
    tm = TOKEN_TILE
    block_bytes = 2 * tm * d * 4 + 2 * tm * n * 2 + w[0].size * 2 + n * 4
    temp_bytes = tm * d * 2 + 4 * tm * PROJ_CHUNK * 4
    return pl.pallas_call(
        _proj_body,
        grid=(t // tm,),
        in_specs=[
            pl.BlockSpec((tm, d), lambda i: (i, 0)),
            _layer_block(w.shape, layer),
            _resident((1, n)),
        ],
        out_specs=pl.BlockSpec((tm, n), lambda i: (i, 0)),
        out_shape=jax.ShapeDtypeStruct((t, n), BF16),
        compiler_params=pltpu.CompilerParams(
            dimension_semantics=("parallel",),
            vmem_limit_bytes=_vmem_limit(block_bytes, temp_bytes)),
        name="proj",
    )(x, w, bias.reshape(1, n))


def _out_ln_body(y_ref, w_ref, bias_ref, x_ref, g_ref, b_ref, o_ref):
    mix = _mm(y_ref[...], w_ref[...]) + bias_ref[...]
    o_ref[...] = _layer_norm(DEEPNORM_ALPHA * x_ref[...] + mix, g_ref[...], b_ref[...])


def _out_ln(y, w, layer, bias, x, g, b):
    t, k = y.shape
    d = w.shape[-1]
    tm = TOKEN_TILE
    block_bytes = 2 * tm * k * 2 + 2 * 2 * tm * d * 4 + w[0].size * 2
    temp_bytes = 4 * tm * d * 4
    return pl.pallas_call(
        _out_ln_body,
        grid=(t // tm,),
        in_specs=[
            pl.BlockSpec((tm, k), lambda i: (i, 0)),
            _layer_block(w.shape, layer),
            _resident((1, d)),
            pl.BlockSpec((tm, d), lambda i: (i, 0)),
            _resident((1, d)),
            _resident((1, d)),
        ],
        out_specs=pl.BlockSpec((tm, d), lambda i: (i, 0)),
        out_shape=jax.ShapeDtypeStruct((t, d), F32),
        compiler_params=pltpu.CompilerParams(
            dimension_semantics=("parallel",),
            vmem_limit_bytes=_vmem_limit(block_bytes, temp_bytes)),
        name="out_ln",
    )(y, w, bias.reshape(1, d), x, g.reshape(1, d), b.reshape(1, d))


ATTN_PAIR_LANES = 2 * ATTN_HEAD_DIM
ATTN_PAIRS_PER_KV = ATTN_GROUP // 2
ATTN_KV_DUP = ATTN_KV_HEADS * ATTN_PAIR_LANES
ATTN_PROJ_DIM = ATTN_Q_DIM + 2 * ATTN_KV_DUP
assert WINDOW == ATTN_BLOCK and ATTN_PAIR_LANES == V7X_LANES and ATTN_GROUP % 2 == 0


def _attn_body(sink_ref, q_ref, kv_ref, kvp_ref, o_ref):
    blk = ATTN_BLOCK
    first = pl.program_id(1) == 0
    lane = lax.broadcasted_iota(jnp.int32, (2 * blk, ATTN_PAIR_LANES), 1)
    low = lane < ATTN_HEAD_DIM
    qi = lax.broadcasted_iota(jnp.int32, (blk, blk), 0)
    kc = lax.broadcasted_iota(jnp.int32, (blk, blk), 1)
    from_prev = kc > qi + (blk - WINDOW)
    lane_q = lax.broadcasted_iota(jnp.int32, (blk, ATTN_PAIR_LANES), 1) < ATTN_HEAD_DIM
    zero = jnp.zeros((), BF16)

    for j in range(ATTN_KV_HEADS):
        kk = jnp.concatenate([kvp_ref[:, j * ATTN_PAIR_LANES:(j + 1) * ATTN_PAIR_LANES],
                              kv_ref[:, j * ATTN_PAIR_LANES:(j + 1) * ATTN_PAIR_LANES]], axis=0)
        vo = ATTN_KV_DUP + j * ATTN_PAIR_LANES
        vv = jnp.concatenate([kvp_ref[:, vo:vo + ATTN_PAIR_LANES], kv_ref[:, vo:vo + ATTN_PAIR_LANES]], axis=0)
        k_bd = jnp.concatenate([jnp.where(low, kk, zero), jnp.where(low, zero, kk)], axis=0)
        v_bd = jnp.concatenate([jnp.where(low, vv, zero), jnp.where(low, zero, vv)], axis=0)
        qo = j * ATTN_GROUP * ATTN_HEAD_DIM
        q_st = jnp.concatenate(
            [q_ref[:, qo + p * ATTN_PAIR_LANES:qo + (p + 1) * ATTN_PAIR_LANES] for p in range(ATTN_PAIRS_PER_KV)],
            axis=0)
        s = lax.dot_general(q_st, k_bd, (((1,), (1,)), ((), ())), preferred_element_type=F32)
        s = s * (ATTN_HEAD_DIM ** -0.5)
        p_rows = []
        inv = []
        for p in range(ATTN_PAIRS_PER_KV):
            tiles = []
            for par in range(2):
                head = j * ATTN_GROUP + 2 * p + par
                sink = sink_ref[head]
                base = par * 2 * blk
                s_prev = s[p * blk:(p + 1) * blk, base:base + blk]
                s_cur = s[p * blk:(p + 1) * blk, base + blk:base + 2 * blk]
                s_prev = jnp.where(first, NEG_INF, s_prev)
                logit = jnp.where(from_prev, s_prev, s_cur)
                m = jnp.maximum(jnp.max(logit, axis=-1, keepdims=True), sink)
                e = jnp.exp(logit - m)
                denom = jnp.sum(e, axis=-1, keepdims=True) + jnp.exp(sink - m)
                inv.append(1.0 / denom)
                eb = e.astype(BF16)
                tiles += [jnp.where(from_prev, eb, zero), jnp.where(from_prev, zero, eb)]
            p_rows.append(jnp.concatenate(tiles, axis=1))
        p_all = jnp.concatenate(p_rows, axis=0)
        o = _mm(p_all, v_bd)
        for p in range(ATTN_PAIRS_PER_KV):
            scale = jnp.where(lane_q, inv[2 * p], inv[2 * p + 1])
            o_ref[:, qo + p * ATTN_PAIR_LANES:qo + (p + 1) * ATTN_PAIR_LANES] = (
                o[p * blk:(p + 1) * blk, :] * scale).astype(o_ref.dtype)


def _attn_core(qkv, sinks, batch, seq):
    nb = seq // ATTN_BLOCK
    qkv = qkv.reshape(batch, seq, ATTN_PROJ_DIM)
    kv_w = 2 * ATTN_KV_DUP
    kv_col = ATTN_Q_DIM // kv_w
    block_bytes = 2 * ATTN_BLOCK * (2 * ATTN_Q_DIM + 2 * kv_w) * 2
    temp_bytes = 24 * ATTN_BLOCK * ATTN_Q_DIM * 4
    out = pl.pallas_call(
        _attn_body,
        grid=(batch, nb),
        in_specs=[
            pl.BlockSpec(memory_space=pltpu.SMEM),
            pl.BlockSpec((None, ATTN_BLOCK, ATTN_Q_DIM), lambda b, n: (b, n, 0)),
            pl.BlockSpec((None, ATTN_BLOCK, kv_w), lambda b, n: (b, n, kv_col)),
            pl.BlockSpec((None, ATTN_BLOCK, kv_w), lambda b, n: (b, jnp.maximum(n - 1, 0), kv_col)),
        ],
        out_specs=pl.BlockSpec((None, ATTN_BLOCK, ATTN_Q_DIM), lambda b, n: (b, n, 0)),
        out_shape=jax.ShapeDtypeStruct((batch, seq, ATTN_Q_DIM), BF16),
        compiler_params=pltpu.CompilerParams(
            dimension_semantics=("parallel", "parallel"),
            vmem_limit_bytes=_vmem_limit(block_bytes, temp_bytes)),
        name="swa_core",
    )(sinks, qkv, qkv, qkv)
    return out.reshape(batch * seq, ATTN_Q_DIM)


def _attn_qkv_layout(w_qkv, b_qkv):
    def dup(t, off):
        parts = []
        for j in range(ATTN_KV_HEADS):
            col = t[..., off + j * ATTN_HEAD_DIM:off + (j + 1) * ATTN_HEAD_DIM]
            parts += [col, col]
        return parts
    def relayout(t):
        return jnp.concatenate([t[..., :ATTN_Q_DIM]] + dup(t, ATTN_Q_DIM) + dup(t, ATTN_Q_DIM + ATTN_KV_DIM), axis=-1)
    return relayout(w_qkv), relayout(b_qkv)


RET_STEP = V7X_MXU_DIM


def _ret_layer_body(x_ref, w_ref, wo_ref, cos_ref, sin_ref, g_ref, b_ref, o_ref, state_ref):
    c = RET_STEP

    @pl.when(pl.program_id(1) == 0)
    def _():
        state_ref[...] = jnp.zeros_like(state_ref)

    x = x_ref[...]
    xb = x.astype(BF16)
    cos = cos_ref[...]
    sin = sin_ref[...]
    row = lax.broadcasted_iota(jnp.int32, (c, c), 0).astype(F32)
    col = lax.broadcasted_iota(jnp.int32, (c, c), 1).astype(F32)
    diff = row - col
    idx = lax.broadcasted_iota(jnp.int32, (c, 1), 0).astype(F32)

    def rot(t):
        t1 = t[:, :ROPE_HALF]
        t2 = t[:, ROPE_HALF:]
        return jnp.concatenate([t1 * cos - t2 * sin, t1 * sin + t2 * cos], axis=-1)

    mix = None
    for h in range(RET_HEADS):
        lg = RET_LOG_GAMMA[h]
        q_lo = h * RET_QK_DIM
        k_lo = RET_QK_ALL + h * RET_QK_DIM
        v_lo = 2 * RET_QK_ALL + h * RET_V_DIM
        g_lo = 2 * RET_QK_ALL + RET_V_ALL + h * RET_V_DIM
        q = rot(_mm(xb, w_ref[:, q_lo:q_lo + RET_QK_DIM]))
        k = rot(_mm(xb, w_ref[:, k_lo:k_lo + RET_QK_DIM])) * (RET_QK_DIM ** -0.5)
        v = _mm(xb, w_ref[:, v_lo:v_lo + RET_V_DIM]).astype(BF16)
        gate = _mm(xb, w_ref[:, g_lo:g_lo + RET_V_DIM])
        decay_in = jnp.where(diff >= 0, jnp.exp(jnp.maximum(diff, 0.0) * lg), 0.0)
        q_decay = jnp.exp((idx + 1.0) * lg)
        k_decay = jnp.exp((c - 1.0 - idx) * lg)
        chunk_decay = math.exp(c * lg)

        qk = lax.dot_general(q.astype(BF16), k.astype(BF16), (((1,), (1,)), ((), ())),
                             preferred_element_type=F32) * decay_in
        inner = _mm(qk.astype(BF16), v)
        state = state_ref[h]
        cross = _mm((q * q_decay).astype(BF16), state.astype(BF16))
        kd = (k * k_decay).astype(BF16)
        kv = lax.dot_general(kd, v, (((0,), (0,)), ((), ())), preferred_element_type=F32)
        state_ref[h] = state * chunk_decay + kv

        y = inner + cross
        mu = jnp.mean(y, axis=-1, keepdims=True)
        yc = y - mu
        var = jnp.mean(yc * yc, axis=-1, keepdims=True)
        yn = yc * lax.rsqrt(var + GN_EPS)
        gated = (gate * jax.nn.sigmoid(gate) * yn).astype(BF16)
        part = _mm(gated, wo_ref[h * RET_V_DIM:(h + 1) * RET_V_DIM, :])
        mix = part if mix is None else mix + part

    o_ref[...] = _layer_norm(DEEPNORM_ALPHA * x + mix, g_ref[...], b_ref[...])


def _ret_layer(x, w, wo, layer, cos, sin, g, b, batch, seq):
    d = x.shape[-1]
    c = RET_STEP
    x = x.reshape(batch, seq, d)
    block_bytes = 2 * 2 * c * d * 4 + (w[0].size + wo[0].size) * 2 + 2 * 2 * c * ROPE_HALF * 4
    temp_bytes = RET_HEADS * RET_QK_DIM * RET_V_DIM * 4 + 8 * c * RET_PROJ_DIM * 4
    out = pl.pallas_call(
        _ret_layer_body,
        grid=(batch, seq // c),
        in_specs=[
            pl.BlockSpec((None, c, d), lambda bi, n: (bi, n, 0)),
            _layer_block(w.shape, layer),
            _layer_block(wo.shape, layer),
            pl.BlockSpec((c, ROPE_HALF), lambda bi, n: (n, 0)),
            pl.BlockSpec((c, ROPE_HALF), lambda bi, n: (n, 0)),
            _resident((1, d)),
            _resident((1, d)),
        ],
        out_specs=pl.BlockSpec((None, c, d), lambda bi, n: (bi, n, 0)),
        out_shape=jax.ShapeDtypeStruct((batch, seq, d), F32),
        scratch_shapes=[pltpu.VMEM((RET_HEADS, RET_QK_DIM, RET_V_DIM), F32)],
        compiler_params=pltpu.CompilerParams(
            dimension_semantics=("parallel", "arbitrary"),
            vmem_limit_bytes=_vmem_limit(block_bytes, temp_bytes)),
        name="retention_layer",
    )(x, w, wo, cos, sin, g.reshape(1, d), b.reshape(1, d))
    return out.reshape(batch * seq, d)


def _rope_tables(seq):
    pos = jnp.arange(seq, dtype=F32)
    freqs = 1.0 / (ROPE_BASE ** jnp.linspace(0.0, 1.0, ROPE_HALF, dtype=F32))
    ang = pos[:, None] * freqs[None, :]
    return jnp.cos(ang), jnp.sin(ang)


def kernel(x, ln_g, ln_b, ffn1_w_gate_up, ffn1_w_down, ffn2_w_gate_up, ffn2_w_down,
           attn_w_qkv, attn_b_qkv, attn_sinks, attn_w_o, attn_b_o, ret_w_qkvg, ret_w_o):
    batch, seq, d = x.shape
    assert d == D_MODEL and seq % ATTN_BLOCK == 0 and seq % RET_STEP == 0 and (batch * seq) % TOKEN_TILE == 0
    cos, sin = _rope_tables(seq)
    w_qkv, b_qkv = _attn_qkv_layout(attn_w_qkv, attn_b_qkv)
    w_qkv = w_qkv.astype(BF16)
    attn_wo = attn_w_o.astype(BF16)
    ret_w = ret_w_qkvg.astype(BF16)
    ret_wo = ret_w_o.astype(BF16)
    h = x.reshape(batch * seq, d)
    for i in range(DEPTH):
        h = _ffn_ln(h, ffn1_w_gate_up, ffn1_w_down, i, ln_g[i, 0], ln_b[i, 0])
        j = i // N_MIXERS
        if i % N_MIXERS == 0:
            qkv = _proj(h, w_qkv, j, b_qkv[j])
            y = _attn_core(qkv, attn_sinks[j], batch, seq)
            h = _out_ln(y, attn_wo, j, attn_b_o[j], h, ln_g[i, 1], ln_b[i, 1])
        else:
            h = _ret_layer(h, ret_w, ret_wo, j, cos, sin, ln_g[i, 1], ln_b[i, 1], batch, seq)
        h = _ffn_ln(h, ffn2_w_gate_up, ffn2_w_down, i, ln_g[i, 2], ln_b[i, 2])
    return h.reshape(batch, seq, d)
```

```python
import math

import jax
import jax.numpy as jnp
from jax import lax
from jax.experimental import pallas as pl
from jax.experimental.pallas import tpu as pltpu

D_MODEL = 1024
DEPTH = 4
N_MIXERS = 2
ATTN_Q_HEADS = 16
ATTN_KV_HEADS = 2
ATTN_HEAD_DIM = 64
WINDOW = 128
ATTN_BLOCK = 128
RET_HEADS = 4
RET_QK_DIM = D_MODEL // RET_HEADS
RET_V_DIM = 2 * D_MODEL // RET_HEADS
ROPE_BASE = 10000.0
FFN_DIM = 2816
DEEPNORM_ALPHA = (2.0 * DEPTH) ** 0.25
LN_EPS = 1e-5
GN_EPS = 1e-6
NEG_INF = -1e30

ATTN_GROUP = ATTN_Q_HEADS // ATTN_KV_HEADS
ATTN_Q_DIM = ATTN_Q_HEADS * ATTN_HEAD_DIM
ATTN_KV_DIM = ATTN_KV_HEADS * ATTN_HEAD_DIM
RET_QK_ALL = RET_HEADS * RET_QK_DIM
RET_V_ALL = RET_HEADS * RET_V_DIM
RET_PROJ_DIM = 2 * RET_QK_ALL + 2 * RET_V_ALL
ROPE_HALF = RET_QK_DIM // 2
RET_LOG_GAMMA = tuple(math.log(1.0 - 2.0 ** (-5.0 - h)) for h in range(RET_HEADS))

V7X_LANES = 128
V7X_MXU_DIM = 256
V7X_VMEM_BYTES = 64 * 1024 * 1024

TOKEN_TILE = 512
FFN_CHUNK = V7X_MXU_DIM
RET_STEP = V7X_MXU_DIM

F32 = jnp.float32
BF16 = jnp.bfloat16


def _vmem_limit(block_bytes, temp_bytes):
    want = block_bytes + temp_bytes
    return int(min(want, V7X_VMEM_BYTES - 8 * 1024 * 1024))


def _resident(shape):
    return pl.BlockSpec(shape, lambda *_: (0,) * len(shape), pipeline_mode=pl.Buffered(1))


def _layer_block(shape, layer):
    return pl.BlockSpec((None,) + tuple(shape[1:]), lambda *_: (layer,) + (0,) * (len(shape) - 1),
                        pipeline_mode=pl.Buffered(1))


def _layer_norm(y, g, b):
    mu = jnp.mean(y, axis=-1, keepdims=True)
    yc = y - mu
    var = jnp.mean(yc * yc, axis=-1, keepdims=True)
    return yc * lax.rsqrt(var + LN_EPS) * g + b


def _mm(a, b):
    return jnp.dot(a, b, preferred_element_type=F32)


def _mm_nt(a, b):
    return lax.dot_general(a, b, (((1,), (1,)), ((), ())), preferred_element_type=F32)


def _mm_tn(a, b):
    return lax.dot_general(a, b, (((0,), (0,)), ((), ())), preferred_element_type=F32)


def _ffn_ln_body(x_ref, wgu_ref, wd_ref, g_ref, b_ref, o_ref):
    x = x_ref[...]
    xb = x.astype(BF16)
    acc = None
    for c in range(FFN_DIM // FFN_CHUNK):
        lo = c * FFN_CHUNK
        gate = _mm(xb, wgu_ref[:, lo:lo + FFN_CHUNK].astype(BF16))
        up = _mm(xb, wgu_ref[:, FFN_DIM + lo:FFN_DIM + lo + FFN_CHUNK].astype(BF16))
        h = (gate * jax.nn.sigmoid(gate) * up).astype(BF16)
        part = _mm(h, wd_ref[lo:lo + FFN_CHUNK, :].astype(BF16))
        acc = part if acc is None else acc + part
    y = DEEPNORM_ALPHA * x + 0.5 * acc
    o_ref[...] = _layer_norm(y, g_ref[...], b_ref[...])


def _ffn_ln(x, w_gate_up, w_down, layer, g, b):
    t, d = x.shape
    tm = TOKEN_TILE
    block_bytes = 2 * 2 * tm * d * 4 + (w_gate_up[0].size + w_down[0].size) * 4
    temp_bytes = 6 * tm * d * 4
    return pl.pallas_call(
        _ffn_ln_body,
        grid=(t // tm,),
        in_specs=[
            pl.BlockSpec((tm, d), lambda i: (i, 0)),
            _layer_block(w_gate_up.shape, layer),
            _layer_block(w_down.shape, layer),
            _resident((1, d)),
            _resident((1, d)),
        ],
        out_specs=pl.BlockSpec((tm, d), lambda i: (i, 0)),
        out_shape=jax.ShapeDtypeStruct((t, d), F32),
        compiler_params=pltpu.CompilerParams(
            dimension_semantics=("parallel",),
            vmem_limit_bytes=_vmem_limit(block_bytes, temp_bytes)),
        name="ffn_ln",
    )(x, w_gate_up, w_down, g.reshape(1, d), b.reshape(1, d))


ATTN_PAIR_LANES = 2 * ATTN_HEAD_DIM
ATTN_PAIRS_PER_KV = ATTN_GROUP // 2
ATTN_KV_DUP = ATTN_KV_HEADS * ATTN_PAIR_LANES
ATTN_PROJ_DIM = ATTN_Q_DIM + 2 * ATTN_KV_DUP
assert WINDOW == ATTN_BLOCK and ATTN_PAIR_LANES == V7X_LANES and ATTN_GROUP % 2 == 0


def _swa_block(q, kv_cur, kv_prev, sink_ref, first):
    blk = ATTN_BLOCK
    lane = lax.broadcasted_iota(jnp.int32, (2 * blk, ATTN_PAIR_LANES), 1)
    low = lane < ATTN_HEAD_DIM
    qi = lax.broadcasted_iota(jnp.int32, (blk, blk), 0)
    kc = lax.broadcasted_iota(jnp.int32, (blk, blk), 1)
    from_prev = kc > qi
    lane_q = lax.broadcasted_iota(jnp.int32, (blk, ATTN_PAIR_LANES), 1) < ATTN_HEAD_DIM
    zero = jnp.zeros((), BF16)
    out_tiles = []
    for j in range(ATTN_KV_HEADS):
        ko = j * ATTN_PAIR_LANES
        vo = ATTN_KV_DUP + j * ATTN_PAIR_LANES
        kk = jnp.concatenate([kv_prev[:, ko:ko + ATTN_PAIR_LANES], kv_cur[:, ko:ko + ATTN_PAIR_LANES]], axis=0)
        vv = jnp.concatenate([kv_prev[:, vo:vo + ATTN_PAIR_LANES], kv_cur[:, vo:vo + ATTN_PAIR_LANES]], axis=0)
        k_bd = jnp.concatenate([jnp.where(low, kk, zero), jnp.where(low, zero, kk)], axis=0)
        v_bd = jnp.concatenate([jnp.where(low, vv, zero), jnp.where(low, zero, vv)], axis=0)
        qo = j * ATTN_GROUP * ATTN_HEAD_DIM
        q_st = jnp.concatenate(
            [q[:, qo + p * ATTN_PAIR_LANES:qo + (p + 1) * ATTN_PAIR_LANES] for p in range(ATTN_PAIRS_PER_KV)],
            axis=0)
        s = _mm_nt(q_st, k_bd)
        p_rows = []
        inv = []
        for p in range(ATTN_PAIRS_PER_KV):
            tiles = []
            for par in range(2):
                sink = sink_ref[j * ATTN_GROUP + 2 * p + par]
                base = par * 2 * blk
                s_prev = s[p * blk:(p + 1) * blk, base:base + blk]
                s_cur = s[p * blk:(p + 1) * blk, base + blk:base + 2 * blk]
                if first is not None:
                    s_prev = jnp.where(first, NEG_INF, s_prev)
                logit = jnp.where(from_prev, s_prev, s_cur)
                m = jnp.maximum(jnp.max(logit, axis=-1, keepdims=True), sink)
                e = jnp.exp(logit - m)
                denom = jnp.sum(e, axis=-1, keepdims=True) + jnp.exp(sink - m)
                inv.append(1.0 / denom)
                eb = e.astype(BF16)
                tiles += [jnp.where(from_prev, eb, zero), jnp.where(from_prev, zero, eb)]
            p_rows.append(jnp.concatenate(tiles, axis=1))
        o = _mm(jnp.concatenate(p_rows, axis=0), v_bd)
        for p in range(ATTN_PAIRS_PER_KV):
            scale = jnp.where(lane_q, inv[2 * p], inv[2 * p + 1])
            out_tiles.append((o[p * blk:(p + 1) * blk, :] * scale).astype(BF16))
    return jnp.concatenate(out_tiles, axis=1)


def _attn_layer_body(sink_ref, x_ref, w_ref, bqkv_ref, wo_ref, bo_ref, g_ref, b_ref, o_ref, kv_ref):
    blk = ATTN_BLOCK
    tile = x_ref.shape[0]
    at_start = pl.program_id(1) == 0

    @pl.when(at_start)
    def _():
        kv_ref[...] = jnp.zeros_like(kv_ref)

    x = x_ref[...]
    xb = x.astype(BF16)
    q = ((_mm(xb, w_ref[:, :ATTN_Q_DIM]) + bqkv_ref[:, :ATTN_Q_DIM]) * (ATTN_HEAD_DIM ** -0.5)).astype(BF16)
    kv = (_mm(xb, w_ref[:, ATTN_Q_DIM:]) + bqkv_ref[:, ATTN_Q_DIM:]).astype(BF16)
    heads = []
    for r in range(tile // blk):
        rows = slice(r * blk, (r + 1) * blk)
        kv_prev = kv_ref[...].astype(BF16) if r == 0 else kv[(r - 1) * blk:r * blk]
        heads.append(_swa_block(q[rows], kv[rows], kv_prev, sink_ref, at_start if r == 0 else None))
    kv_ref[...] = kv[tile - blk:].astype(F32)
    mix = _mm(jnp.concatenate(heads, axis=0), wo_ref[...]) + bo_ref[...]
    o_ref[...] = _layer_norm(DEEPNORM_ALPHA * x + mix, g_ref[...], b_ref[...])


def _attn_layer(x, w_qkv, b_qkv, sinks, wo, bo, layer, g, b, batch, seq):
    d = x.shape[-1]
    tile = TOKEN_TILE
    x = x.reshape(batch, seq, d)
    block_bytes = 2 * 2 * tile * d * 4 + (w_qkv[0].size + wo[0].size) * 2
    temp_bytes = 12 * tile * d * 4
    out = pl.pallas_call(
        _attn_layer_body,
        grid=(batch, seq // tile),
        in_specs=[
            pl.BlockSpec(memory_space=pltpu.SMEM),
            pl.BlockSpec((None, tile, d), lambda bi, n: (bi, n, 0)),
            _layer_block(w_qkv.shape, layer),
            _resident((1, ATTN_PROJ_DIM)),
            _layer_block(wo.shape, layer),
            _resident((1, d)),
            _resident((1, d)),
            _resident((1, d)),
        ],
        out_specs=pl.BlockSpec((None, tile, d), lambda bi, n: (bi, n, 0)),
        out_shape=jax.ShapeDtypeStruct((batch, seq, d), F32),
        scratch_shapes=[pltpu.VMEM((ATTN_BLOCK, 2 * ATTN_KV_DUP), F32)],
        compiler_params=pltpu.CompilerParams(
            dimension_semantics=("parallel", "arbitrary"),
            vmem_limit_bytes=_vmem_limit(block_bytes, temp_bytes)),
        name="attention_layer",
    )(sinks, x, w_qkv, b_qkv.reshape(1, ATTN_PROJ_DIM), wo, bo.reshape(1, d), g.reshape(1, d), b.reshape(1, d))
    return out.reshape(batch * seq, d)


def _attn_qkv_layout(w_qkv, b_qkv):
    def dup(t, off):
        parts = []
        for j in range(ATTN_KV_HEADS):
            col = t[..., off + j * ATTN_HEAD_DIM:off + (j + 1) * ATTN_HEAD_DIM]
            parts += [col, col]
        return parts
    def relayout(t):
        return jnp.concatenate([t[..., :ATTN_Q_DIM]] + dup(t, ATTN_Q_DIM) + dup(t, ATTN_Q_DIM + ATTN_KV_DIM), axis=-1)
    return relayout(w_qkv), relayout(b_qkv)


def _ret_chunk(x, cos, sin, w_ref, wo_ref, state_ref):
    c = RET_STEP
    xb = x.astype(BF16)
    row = lax.broadcasted_iota(jnp.int32, (c, c), 0).astype(F32)
    col = lax.broadcasted_iota(jnp.int32, (c, c), 1).astype(F32)
    diff = row - col
    idx = lax.broadcasted_iota(jnp.int32, (c, 1), 0).astype(F32)

    def rot(t):
        t1 = t[:, :ROPE_HALF]
        t2 = t[:, ROPE_HALF:]
        return jnp.concatenate([t1 * cos - t2 * sin, t1 * sin + t2 * cos], axis=-1)

    mix = None
    for h in range(RET_HEADS):
        lg = RET_LOG_GAMMA[h]
        q_lo = h * RET_QK_DIM
        k_lo = RET_QK_ALL + h * RET_QK_DIM
        v_lo = 2 * RET_QK_ALL + h * RET_V_DIM
        g_lo = 2 * RET_QK_ALL + RET_V_ALL + h * RET_V_DIM
        q = rot(_mm(xb, w_ref[:, q_lo:q_lo + RET_QK_DIM]))
        k = rot(_mm(xb, w_ref[:, k_lo:k_lo + RET_QK_DIM])) * (RET_QK_DIM ** -0.5)
        v = _mm(xb, w_ref[:, v_lo:v_lo + RET_V_DIM]).astype(BF16)
        gate = _mm(xb, w_ref[:, g_lo:g_lo + RET_V_DIM])
        decay_in = jnp.where(diff >= 0, jnp.exp(jnp.maximum(diff, 0.0) * lg), 0.0)
        q_decay = jnp.exp((idx + 1.0) * lg)
        k_decay = jnp.exp((c - 1.0 - idx) * lg)
        chunk_decay = math.exp(c * lg)

        qk = _mm_nt(q.astype(BF16), k.astype(BF16)) * decay_in
        inner = _mm(qk.astype(BF16), v)
        state = state_ref[h]
        cross = _mm((q * q_decay).astype(BF16), state.astype(BF16))
        kv = _mm_tn((k * k_decay).astype(BF16), v)
        state_ref[h] = state * chunk_decay + kv

        y = inner + cross
        mu = jnp.mean(y, axis=-1, keepdims=True)
        yc = y - mu
        var = jnp.mean(yc * yc, axis=-1, keepdims=True)
        yn = yc * lax.rsqrt(var + GN_EPS)
        gated = (gate * jax.nn.sigmoid(gate) * yn).astype(BF16)
        part = _mm(gated, wo_ref[h * RET_V_DIM:(h + 1) * RET_V_DIM, :])
        mix = part if mix is None else mix + part
    return mix


def _ret_layer_body(x_ref, w_ref, wo_ref, cos_ref, sin_ref, g_ref, b_ref, o_ref, state_ref):
    c = RET_STEP

    @pl.when(pl.program_id(1) == 0)
    def _():
        state_ref[...] = jnp.zeros_like(state_ref)

    for r in range(x_ref.shape[0] // c):
        rows = slice(r * c, (r + 1) * c)
        x = x_ref[rows, :]
        mix = _ret_chunk(x, cos_ref[rows, :], sin_ref[rows, :], w_ref, wo_ref, state_ref)
        o_ref[rows, :] = _layer_norm(DEEPNORM_ALPHA * x + mix, g_ref[...], b_ref[...])


def _ret_layer(x, w, wo, layer, cos, sin, g, b, batch, seq):
    d = x.shape[-1]
    tile = TOKEN_TILE
    x = x.reshape(batch, seq, d)
    block_bytes = 2 * 2 * tile * d * 4 + (w[0].size + wo[0].size) * 2 + 2 * 2 * tile * ROPE_HALF * 4
    temp_bytes = RET_HEADS * RET_QK_DIM * RET_V_DIM * 4 + 8 * tile * RET_PROJ_DIM * 4
    out = pl.pallas_call(
        _ret_layer_body,
        grid=(batch, seq // tile),
        in_specs=[
            pl.BlockSpec((None, tile, d), lambda bi, n: (bi, n, 0)),
            _layer_block(w.shape, layer),
            _layer_block(wo.shape, layer),
            pl.BlockSpec((tile, ROPE_HALF), lambda bi, n: (n, 0)),
            pl.BlockSpec((tile, ROPE_HALF), lambda bi, n: (n, 0)),
            _resident((1, d)),
            _resident((1, d)),
        ],
        out_specs=pl.BlockSpec((None, tile, d), lambda bi, n: (bi, n, 0)),
        out_shape=jax.ShapeDtypeStruct((batch, seq, d), F32),
        scratch_shapes=[pltpu.VMEM((RET_HEADS, RET_QK_DIM, RET_V_DIM), F32)],
        compiler_params=pltpu.CompilerParams(
            dimension_semantics=("parallel", "arbitrary"),
            vmem_limit_bytes=_vmem_limit(block_bytes, temp_bytes)),
        name="retention_layer",
    )(x, w, wo, cos, sin, g.reshape(1, d), b.reshape(1, d))
    return out.reshape(batch * seq, d)


def _rope_tables(seq):
    pos = jnp.arange(seq, dtype=F32)
    freqs = 1.0 / (ROPE_BASE ** jnp.linspace(0.0, 1.0, ROPE_HALF, dtype=F32))
    ang = pos[:, None] * freqs[None, :]
    return jnp.cos(ang), jnp.sin(ang)


def kernel(x, ln_g, ln_b, ffn1_w_gate_up, ffn1_w_down, ffn2_w_gate_up, ffn2_w_down,
           attn_w_qkv, attn_b_qkv, attn_sinks, attn_w_o, attn_b_o, ret_w_qkvg, ret_w_o):
    batch, seq, d = x.shape
    assert d == D_MODEL and seq % TOKEN_TILE == 0 and TOKEN_TILE % ATTN_BLOCK == 0 and TOKEN_TILE % RET_STEP == 0
    cos, sin = _rope_tables(seq)
    w_qkv, b_qkv = _attn_qkv_layout(attn_w_qkv, attn_b_qkv)
    w_qkv = w_qkv.astype(BF16)
    attn_wo = attn_w_o.astype(BF16)
    ret_w = ret_w_qkvg.astype(BF16)
    ret_wo = ret_w_o.astype(BF16)
    h = x.reshape(batch * seq, d)
    for i in range(DEPTH):
        h = _ffn_ln(h, ffn1_w_gate_up, ffn1_w_down, i, ln_g[i, 0], ln_b[i, 0])
        j = i // N_MIXERS
        if i % N_MIXERS == 0:
            h = _attn_layer(h, w_qkv, b_qkv[j], attn_sinks[j], attn_wo, attn_b_o[j], j,
                            ln_g[i, 1], ln_b[i, 1], batch, seq)
        else:
            h = _ret_layer(h, ret_w, ret_wo, j, cos, sin, ln_g[i, 1], ln_b[i, 1], batch, seq)
        h = _ffn_ln(h, ffn2_w_gate_up, ffn2_w_down, i, ln_g[i, 2], ln_b[i, 2])
    return h.reshape(batch, seq, d)
```

```python
import math

import jax
import jax.numpy as jnp
from jax import lax
from jax.experimental import pallas as pl
from jax.experimental.pallas import tpu as pltpu

D_MODEL = 1024
DEPTH = 4
N_MIXERS = 2
ATTN_Q_HEADS = 16
ATTN_KV_HEADS = 2
ATTN_HEAD_DIM = 64
WINDOW = 128
ATTN_BLOCK = 128
RET_HEADS = 4
RET_QK_DIM = D_MODEL // RET_HEADS
RET_V_DIM = 2 * D_MODEL // RET_HEADS
ROPE_BASE = 10000.0
FFN_DIM = 2816
DEEPNORM_ALPHA = (2.0 * DEPTH) ** 0.25
LN_EPS = 1e-5
GN_EPS = 1e-6
NEG_INF = -1e30

ATTN_GROUP = ATTN_Q_HEADS // ATTN_KV_HEADS
ATTN_Q_DIM = ATTN_Q_HEADS * ATTN_HEAD_DIM
ATTN_KV_DIM = ATTN_KV_HEADS * ATTN_HEAD_DIM
RET_QK_ALL = RET_HEADS * RET_QK_DIM
RET_V_ALL = RET_HEADS * RET_V_DIM
RET_PROJ_DIM = 2 * RET_QK_ALL + 2 * RET_V_ALL
ROPE_HALF = RET_QK_DIM // 2
RET_LOG_GAMMA = tuple(math.log(1.0 - 2.0 ** (-5.0 - h)) for h in range(RET_HEADS))

V7X_LANES = 128
V7X_MXU_DIM = 256
V7X_VMEM_BYTES = 64 * 1024 * 1024

TOKEN_TILE = 512
FFN_CHUNK = V7X_MXU_DIM
RET_STEP = V7X_MXU_DIM

F32 = jnp.float32
BF16 = jnp.bfloat16


def _vmem_limit(block_bytes, temp_bytes):
    want = block_bytes + temp_bytes
    return int(min(want, V7X_VMEM_BYTES - 8 * 1024 * 1024))


def _resident(shape):
    return pl.BlockSpec(shape, lambda *_: (0,) * len(shape), pipeline_mode=pl.Buffered(1))


def _layer_block(shape, layer):
    return pl.BlockSpec((None,) + tuple(shape[1:]), lambda *_: (layer,) + (0,) * (len(shape) - 1),
                        pipeline_mode=pl.Buffered(1))


def _layer_norm(y, g, b):
    mu = jnp.mean(y, axis=-1, keepdims=True)
    yc = y - mu
    var = jnp.mean(yc * yc, axis=-1, keepdims=True)
    return yc * lax.rsqrt(var + LN_EPS) * g + b


def _mm(a, b):
    return jnp.dot(a, b, preferred_element_type=F32)


def _mm_nt(a, b):
    return lax.dot_general(a, b, (((1,), (1,)), ((), ())), preferred_element_type=F32)


def _mm_tn(a, b):
    return lax.dot_general(a, b, (((0,), (0,)), ((), ())), preferred_element_type=F32)


def _ffn_ln_body(x_ref, wgu_ref, wd_ref, g_ref, b_ref, o_ref):
    x = x_ref[...]
    xb = x.astype(BF16)
    acc = None
    for c in range(FFN_DIM // FFN_CHUNK):
        lo = c * FFN_CHUNK
        gate = _mm(xb, wgu_ref[:, lo:lo + FFN_CHUNK].astype(BF16))
        up = _mm(xb, wgu_ref[:, FFN_DIM + lo:FFN_DIM + lo + FFN_CHUNK].astype(BF16))
        h = (gate * jax.nn.sigmoid(gate) * up).astype(BF16)
        part = _mm(h, wd_ref[lo:lo + FFN_CHUNK, :].astype(BF16))
        acc = part if acc is None else acc + part
    y = DEEPNORM_ALPHA * x + 0.5 * acc
    o_ref[...] = _layer_norm(y, g_ref[...], b_ref[...])


def _ffn_ln(x, w_gate_up, w_down, layer, g, b):
    t, d = x.shape
    tm = TOKEN_TILE
    block_bytes = 2 * 2 * tm * d * 4 + (w_gate_up[0].size + w_down[0].size) * 4
    temp_bytes = 6 * tm * d * 4
    return pl.pallas_call(
        _ffn_ln_body,
        grid=(t // tm,),
        in_specs=[
            pl.BlockSpec((tm, d), lambda i: (i, 0)),
            _layer_block(w_gate_up.shape, layer),
            _layer_block(w_down.shape, layer),
            _resident((1, d)),
            _resident((1, d)),
        ],
        out_specs=pl.BlockSpec((tm, d), lambda i: (i, 0)),
        out_shape=jax.ShapeDtypeStruct((t, d), F32),
        compiler_params=pltpu.CompilerParams(
            dimension_semantics=("parallel",),
            vmem_limit_bytes=_vmem_limit(block_bytes, temp_bytes)),
        name="ffn_ln",
    )(x, w_gate_up, w_down, g.reshape(1, d), b.reshape(1, d))


ATTN_PAIR_LANES = 2 * ATTN_HEAD_DIM
ATTN_PAIRS_PER_KV = ATTN_GROUP // 2
ATTN_KV_DUP = ATTN_KV_HEADS * ATTN_PAIR_LANES
ATTN_PROJ_DIM = ATTN_Q_DIM + 2 * ATTN_KV_DUP
assert WINDOW == ATTN_BLOCK and ATTN_PAIR_LANES == V7X_LANES and ATTN_GROUP % 2 == 0


def _swa_block(q, kv_cur, kv_prev, sink_ref, first):
    blk = ATTN_BLOCK
    lane = lax.broadcasted_iota(jnp.int32, (2 * blk, ATTN_PAIR_LANES), 1)
    low = lane < ATTN_HEAD_DIM
    qi = lax.broadcasted_iota(jnp.int32, (blk, blk), 0)
    kc = lax.broadcasted_iota(jnp.int32, (blk, blk), 1)
    from_prev = kc > qi
    lane_q = lax.broadcasted_iota(jnp.int32, (blk, ATTN_PAIR_LANES), 1) < ATTN_HEAD_DIM
    zero = jnp.zeros((), BF16)
    out_tiles = []
    for j in range(ATTN_KV_HEADS):
        ko = j * ATTN_PAIR_LANES
        vo = ATTN_KV_DUP + j * ATTN_PAIR_LANES
        kk = jnp.concatenate([kv_prev[:, ko:ko + ATTN_PAIR_LANES], kv_cur[:, ko:ko + ATTN_PAIR_LANES]], axis=0)
        vv = jnp.concatenate([kv_prev[:, vo:vo + ATTN_PAIR_LANES], kv_cur[:, vo:vo + ATTN_PAIR_LANES]], axis=0)
        k_bd = jnp.concatenate([jnp.where(low, kk, zero), jnp.where(low, zero, kk)], axis=0)
        v_bd = jnp.concatenate([jnp.where(low, vv, zero), jnp.where(low, zero, vv)], axis=0)
        qo = j * ATTN_GROUP * ATTN_HEAD_DIM
        q_st = jnp.concatenate(
            [q[:, qo + p * ATTN_PAIR_LANES:qo + (p + 1) * ATTN_PAIR_LANES] for p in range(ATTN_PAIRS_PER_KV)],
            axis=0)
        s = _mm_nt(q_st, k_bd)
        p_rows = []
        inv = []
        for p in range(ATTN_PAIRS_PER_KV):
            tiles = []
            for par in range(2):
                sink = sink_ref[j * ATTN_GROUP + 2 * p + par]
                base = par * 2 * blk
                s_prev = s[p * blk:(p + 1) * blk, base:base + blk]
                s_cur = s[p * blk:(p + 1) * blk, base + blk:base + 2 * blk]
                if first is not None:
                    s_prev = jnp.where(first, NEG_INF, s_prev)
                logit = jnp.where(from_prev, s_prev, s_cur)
                m = jnp.maximum(jnp.max(logit, axis=-1, keepdims=True), sink)
                e = jnp.exp(logit - m)
                denom = jnp.sum(e, axis=-1, keepdims=True) + jnp.exp(sink - m)
                inv.append(1.0 / denom)
                eb = e.astype(BF16)
                tiles += [jnp.where(from_prev, eb, zero), jnp.where(from_prev, zero, eb)]
            p_rows.append(jnp.concatenate(tiles, axis=1))
        o = _mm(jnp.concatenate(p_rows, axis=0), v_bd)
        for p in range(ATTN_PAIRS_PER_KV):
            scale = jnp.where(lane_q, inv[2 * p], inv[2 * p + 1])
            out_tiles.append((o[p * blk:(p + 1) * blk, :] * scale).astype(BF16))
    return jnp.concatenate(out_tiles, axis=1)


def _attn_layer_body(sink_ref, x_ref, w_ref, bqkv_ref, wo_ref, bo_ref, g_ref, b_ref, o_ref, kv_ref):
    blk = ATTN_BLOCK
    tile = x_ref.shape[0]
    at_start = pl.program_id(1) == 0

    @pl.when(at_start)
    def _():
        kv_ref[...] = jnp.zeros_like(kv_ref)

    x = x_ref[...]
    xb = x.astype(BF16)
    q = ((_mm(xb, w_ref[:, :ATTN_Q_DIM]) + bqkv_ref[:, :ATTN_Q_DIM]) * (ATTN_HEAD_DIM ** -0.5)).astype(BF16)
    kv = (_mm(xb, w_ref[:, ATTN_Q_DIM:]) + bqkv_ref[:, ATTN_Q_DIM:]).astype(BF16)
    heads = []
    for r in range(tile // blk):
        rows = slice(r * blk, (r + 1) * blk)
        kv_prev = kv_ref[...].astype(BF16) if r == 0 else kv[(r - 1) * blk:r * blk]
        heads.append(_swa_block(q[rows], kv[rows], kv_prev, sink_ref, at_start if r == 0 else None))
    kv_ref[...] = kv[tile - blk:].astype(F32)
    mix = _mm(jnp.concatenate(heads, axis=0), wo_ref[...]) + bo_ref[...]
    o_ref[...] = _layer_norm(DEEPNORM_ALPHA * x + mix, g_ref[...], b_ref[...])


def _attn_layer(x, w_qkv, b_qkv, sinks, wo, bo, layer, g, b, batch, seq):
    d = x.shape[-1]
    tile = TOKEN_TILE
    x = x.reshape(batch, seq, d)
    block_bytes = 2 * 2 * tile * d * 4 + (w_qkv[0].size + wo[0].size) * 2
    temp_bytes = 12 * tile * d * 4
    out = pl.pallas_call(
        _attn_layer_body,
        grid=(batch, seq // tile),
        in_specs=[
            pl.BlockSpec(memory_space=pltpu.SMEM),
            pl.BlockSpec((None, tile, d), lambda bi, n: (bi, n, 0)),
            _layer_block(w_qkv.shape, layer),
            _resident((1, ATTN_PROJ_DIM)),
            _layer_block(wo.shape, layer),
            _resident((1, d)),
            _resident((1, d)),
            _resident((1, d)),
        ],
        out_specs=pl.BlockSpec((None, tile, d), lambda bi, n: (bi, n, 0)),
        out_shape=jax.ShapeDtypeStruct((batch, seq, d), F32),
        scratch_shapes=[pltpu.VMEM((ATTN_BLOCK, 2 * ATTN_KV_DUP), F32)],
        compiler_params=pltpu.CompilerParams(
            dimension_semantics=("parallel", "arbitrary"),
            vmem_limit_bytes=_vmem_limit(block_bytes, temp_bytes)),
        name="attention_layer",
    )(sinks, x, w_qkv, b_qkv.reshape(1, ATTN_PROJ_DIM), wo, bo.reshape(1, d), g.reshape(1, d), b.reshape(1, d))
    return out.reshape(batch * seq, d)


def _attn_qkv_layout(w_qkv, b_qkv):
    def dup(t, off):
        parts = []
        for j in range(ATTN_KV_HEADS):
            col = t[..., off + j * ATTN_HEAD_DIM:off + (j + 1) * ATTN_HEAD_DIM]
            parts += [col, col]
        return parts
    def relayout(t):
        return jnp.concatenate([t[..., :ATTN_Q_DIM]] + dup(t, ATTN_Q_DIM) + dup(t, ATTN_Q_DIM + ATTN_KV_DIM), axis=-1)
    return relayout(w_qkv), relayout(b_qkv)


def _ret_head_chunk(q, k, v, gate, cos, sin, decay_in, state_ref, h):
    c = RET_STEP
    lg = RET_LOG_GAMMA[h]
    idx = lax.broadcasted_iota(jnp.int32, (c, 1), 0).astype(F32)
    q_decay = jnp.exp((idx + 1.0) * lg)
    k_decay = jnp.exp((c - 1.0 - idx) * lg)
    chunk_decay = math.exp(c * lg)

    def rot(t):
        t1 = t[:, :ROPE_HALF]
        t2 = t[:, ROPE_HALF:]
        return jnp.concatenate([t1 * cos - t2 * sin, t1 * sin + t2 * cos], axis=-1)

    q = rot(q)
    k = rot(k) * (RET_QK_DIM ** -0.5)
    v = v.astype(BF16)
    qk = _mm_nt(q.astype(BF16), k.astype(BF16)) * decay_in
    inner = _mm(qk.astype(BF16), v)
    state = state_ref[h]
    cross = _mm((q * q_decay).astype(BF16), state.astype(BF16))
    kv = _mm_tn((k * k_decay).astype(BF16), v)
    state_ref[h] = state * chunk_decay + kv

    y = inner + cross
    mu = jnp.mean(y, axis=-1, keepdims=True)
    yc = y - mu
    var = jnp.mean(yc * yc, axis=-1, keepdims=True)
    yn = yc * lax.rsqrt(var + GN_EPS)
    return (gate * jax.nn.sigmoid(gate) * yn).astype(BF16)


def _ret_layer_body(x_ref, w_ref, wo_ref, cos_ref, sin_ref, g_ref, b_ref, o_ref, state_ref, decay_ref):
    c = RET_STEP
    n_chunks = x_ref.shape[0] // c

    @pl.when(pl.program_id(1) == 0)
    def _():
        state_ref[...] = jnp.zeros_like(state_ref)
        row = lax.broadcasted_iota(jnp.int32, (c, c), 0).astype(F32)
        col = lax.broadcasted_iota(jnp.int32, (c, c), 1).astype(F32)
        diff = row - col
        for h in range(RET_HEADS):
            decay_ref[h] = jnp.where(diff >= 0, jnp.exp(jnp.maximum(diff, 0.0) * RET_LOG_GAMMA[h]), 0.0)

    x = x_ref[...]
    xb = x.astype(BF16)
    mix = None
    for h in range(RET_HEADS):
        q_lo = h * RET_QK_DIM
        k_lo = RET_QK_ALL + h * RET_QK_DIM
        v_lo = 2 * RET_QK_ALL + h * RET_V_DIM
        g_lo = 2 * RET_QK_ALL + RET_V_ALL + h * RET_V_DIM
        q = _mm(xb, w_ref[:, q_lo:q_lo + RET_QK_DIM].astype(BF16))
        k = _mm(xb, w_ref[:, k_lo:k_lo + RET_QK_DIM].astype(BF16))
        v = _mm(xb, w_ref[:, v_lo:v_lo + RET_V_DIM].astype(BF16))
        gate = _mm(xb, w_ref[:, g_lo:g_lo + RET_V_DIM].astype(BF16))
        decay_in = decay_ref[h]
        gated = []
        for r in range(n_chunks):
            rows = slice(r * c, (r + 1) * c)
            gated.append(_ret_head_chunk(q[rows], k[rows], v[rows], gate[rows], cos_ref[rows, :], sin_ref[rows, :],
                                         decay_in, state_ref, h))
        part = _mm(jnp.concatenate(gated, axis=0), wo_ref[h * RET_V_DIM:(h + 1) * RET_V_DIM, :].astype(BF16))
        mix = part if mix is None else mix + part
    o_ref[...] = _layer_norm(DEEPNORM_ALPHA * x + mix, g_ref[...], b_ref[...])


def _ret_layer(x, w, wo, layer, cos, sin, g, b, batch, seq):
    d = x.shape[-1]
    tile = TOKEN_TILE
    x = x.reshape(batch, seq, d)
    block_bytes = 2 * 2 * tile * d * 4 + (w[0].size + wo[0].size) * 4 + 2 * 2 * tile * ROPE_HALF * 4
    temp_bytes = (RET_HEADS * RET_QK_DIM * RET_V_DIM * 4 + RET_HEADS * RET_STEP * RET_STEP * 4
                  + 2 * tile * RET_PROJ_DIM * 4)
    out = pl.pallas_call(
        _ret_layer_body,
        grid=(batch, seq // tile),
        in_specs=[
            pl.BlockSpec((None, tile, d), lambda bi, n: (bi, n, 0)),
            _layer_block(w.shape, layer),
            _layer_block(wo.shape, layer),
            pl.BlockSpec((tile, ROPE_HALF), lambda bi, n: (n, 0)),
            pl.BlockSpec((tile, ROPE_HALF), lambda bi, n: (n, 0)),
            _resident((1, d)),
            _resident((1, d)),
        ],
        out_specs=pl.BlockSpec((None, tile, d), lambda bi, n: (bi, n, 0)),
        out_shape=jax.ShapeDtypeStruct((batch, seq, d), F32),
        scratch_shapes=[pltpu.VMEM((RET_HEADS, RET_QK_DIM, RET_V_DIM), F32),
                        pltpu.VMEM((RET_HEADS, RET_STEP, RET_STEP), F32)],
        compiler_params=pltpu.CompilerParams(
            dimension_semantics=("parallel", "arbitrary"),
            vmem_limit_bytes=_vmem_limit(block_bytes, temp_bytes)),
        name="retention_layer",
    )(x, w, wo, cos, sin, g.reshape(1, d), b.reshape(1, d))
    return out.reshape(batch * seq, d)


def _rope_tables(seq):
    pos = jnp.arange(seq, dtype=F32)
    freqs = 1.0 / (ROPE_BASE ** jnp.linspace(0.0, 1.0, ROPE_HALF, dtype=F32))
    ang = pos[:, None] * freqs[None, :]
    return jnp.cos(ang), jnp.sin(ang)


def kernel(x, ln_g, ln_b, ffn1_w_gate_up, ffn1_w_down, ffn2_w_gate_up, ffn2_w_down,
           attn_w_qkv, attn_b_qkv, attn_sinks, attn_w_o, attn_b_o, ret_w_qkvg, ret_w_o):
    batch, seq, d = x.shape
    assert d == D_MODEL and seq % TOKEN_TILE == 0 and TOKEN_TILE % ATTN_BLOCK == 0 and TOKEN_TILE % RET_STEP == 0
    cos, sin = _rope_tables(seq)
    w_qkv, b_qkv = _attn_qkv_layout(attn_w_qkv, attn_b_qkv)
    w_qkv = w_qkv.astype(BF16)
    attn_wo = attn_w_o.astype(BF16)
    h = x.reshape(batch * seq, d)
    for i in range(DEPTH):
        h = _ffn_ln(h, ffn1_w_gate_up, ffn1_w_down, i, ln_g[i, 0], ln_b[i, 0])
        j = i // N_MIXERS
        if i % N_MIXERS == 0:
            h = _attn_layer(h, w_qkv, b_qkv[j], attn_sinks[j], attn_wo, attn_b_o[j], j,
                            ln_g[i, 1], ln_b[i, 1], batch, seq)
        else:
            h = _ret_layer(h, ret_w_qkvg, ret_w_o, j, cos, sin, ln_g[i, 1], ln_b[i, 1], batch, seq)
        h = _ffn_ln(h, ffn2_w_gate_up, ffn2_w_down, i, ln_g[i, 2], ln_b[i, 2])
    return h.reshape(batch, seq, d)
```

```python
import functools
import math

import jax
import jax.numpy as jnp
from jax import lax
from jax.experimental import pallas as pl
from jax.experimental.pallas import tpu as pltpu

D_MODEL = 1024
DEPTH = 4
N_MIXERS = 2
ATTN_Q_HEADS = 16
ATTN_KV_HEADS = 2
ATTN_HEAD_DIM = 64
WINDOW = 128
ATTN_BLOCK = 128
RET_HEADS = 4
RET_QK_DIM = D_MODEL // RET_HEADS
RET_V_DIM = 2 * D_MODEL // RET_HEADS
ROPE_BASE = 10000.0
FFN_DIM = 2816
DEEPNORM_ALPHA = (2.0 * DEPTH) ** 0.25
LN_EPS = 1e-5
GN_EPS = 1e-6
NEG_INF = -1e30

ATTN_GROUP = ATTN_Q_HEADS // ATTN_KV_HEADS
ATTN_Q_DIM = ATTN_Q_HEADS * ATTN_HEAD_DIM
ATTN_KV_DIM = ATTN_KV_HEADS * ATTN_HEAD_DIM
RET_QK_ALL = RET_HEADS * RET_QK_DIM
RET_V_ALL = RET_HEADS * RET_V_DIM
RET_PROJ_DIM = 2 * RET_QK_ALL + 2 * RET_V_ALL
ROPE_HALF = RET_QK_DIM // 2
RET_LOG_GAMMA = tuple(math.log(1.0 - 2.0 ** (-5.0 - h)) for h in range(RET_HEADS))

V7X_LANES = 128
V7X_MXU_DIM = 256
V7X_VMEM_BYTES = 64 * 1024 * 1024

TOKEN_TILE = 512
FFN_CHUNK = V7X_MXU_DIM
RET_STEP = V7X_MXU_DIM

F32 = jnp.float32
BF16 = jnp.bfloat16


def _vmem_limit(block_bytes, temp_bytes):
    want = block_bytes + temp_bytes
    return int(min(want, V7X_VMEM_BYTES - 8 * 1024 * 1024))


def _resident(shape):
    return pl.BlockSpec(shape, lambda *_: (0,) * len(shape), pipeline_mode=pl.Buffered(1))


def _layer_block(shape, layer):
    return pl.BlockSpec((None,) + tuple(shape[1:]), lambda *_: (layer,) + (0,) * (len(shape) - 1),
                        pipeline_mode=pl.Buffered(1))


def _layer_norm(y, g, b):
    mu = jnp.mean(y, axis=-1, keepdims=True)
    yc = y - mu
    var = jnp.mean(yc * yc, axis=-1, keepdims=True)
    return yc * lax.rsqrt(var + LN_EPS) * g + b


def _mm(a, b):
    return jnp.dot(a, b, preferred_element_type=F32)


def _mm_nt(a, b):
    return lax.dot_general(a, b, (((1,), (1,)), ((), ())), preferred_element_type=F32)


def _mm_tn(a, b):
    return lax.dot_general(a, b, (((0,), (0,)), ((), ())), preferred_element_type=F32)


def _ffn_ln_body(x_ref, wgu_ref, wd_ref, g_ref, b_ref, o_ref):
    x = x_ref[...]
    xb = x.astype(BF16)
    acc = None
    for c in range(FFN_DIM // FFN_CHUNK):
        lo = c * FFN_CHUNK
        gate = _mm(xb, wgu_ref[:, lo:lo + FFN_CHUNK].astype(BF16))
        up = _mm(xb, wgu_ref[:, FFN_DIM + lo:FFN_DIM + lo + FFN_CHUNK].astype(BF16))
        h = (gate * jax.nn.sigmoid(gate) * up).astype(BF16)
        part = _mm(h, wd_ref[lo:lo + FFN_CHUNK, :].astype(BF16))
        acc = part if acc is None else acc + part
    y = DEEPNORM_ALPHA * x + 0.5 * acc
    o_ref[...] = _layer_norm(y, g_ref[...], b_ref[...])


def _ffn_ln(x, w_gate_up, w_down, layer, g, b):
    t, d = x.shape
    tm = TOKEN_TILE
    block_bytes = 2 * 2 * tm * d * 4 + (w_gate_up[0].size + w_down[0].size) * 4
    temp_bytes = 6 * tm * d * 4
    return pl.pallas_call(
        _ffn_ln_body,
        grid=(t // tm,),
        in_specs=[
            pl.BlockSpec((tm, d), lambda i: (i, 0)),
            _layer_block(w_gate_up.shape, layer),
            _layer_block(w_down.shape, layer),
            _resident((1, d)),
            _resident((1, d)),
        ],
        out_specs=pl.BlockSpec((tm, d), lambda i: (i, 0)),
        out_shape=jax.ShapeDtypeStruct((t, d), F32),
        compiler_params=pltpu.CompilerParams(
            dimension_semantics=("parallel",),
            vmem_limit_bytes=_vmem_limit(block_bytes, temp_bytes)),
        name="ffn_ln",
    )(x, w_gate_up, w_down, g.reshape(1, d), b.reshape(1, d))


ATTN_PAIR_LANES = 2 * ATTN_HEAD_DIM
ATTN_PAIRS_PER_KV = ATTN_GROUP // 2
ATTN_KV_DUP = ATTN_KV_HEADS * ATTN_PAIR_LANES
ATTN_PROJ_DIM = ATTN_Q_DIM + 2 * ATTN_KV_DUP
assert WINDOW == ATTN_BLOCK and ATTN_PAIR_LANES == V7X_LANES and ATTN_GROUP % 2 == 0


def _swa_block(q, kv_cur, kv_prev, sink_ref, first):
    blk = ATTN_BLOCK
    lane = lax.broadcasted_iota(jnp.int32, (2 * blk, ATTN_PAIR_LANES), 1)
    low = lane < ATTN_HEAD_DIM
    qi = lax.broadcasted_iota(jnp.int32, (blk, blk), 0)
    kc = lax.broadcasted_iota(jnp.int32, (blk, blk), 1)
    from_prev = kc > qi
    lane_q = lax.broadcasted_iota(jnp.int32, (blk, ATTN_PAIR_LANES), 1) < ATTN_HEAD_DIM
    zero = jnp.zeros((), BF16)
    out_tiles = []
    for j in range(ATTN_KV_HEADS):
        ko = j * ATTN_PAIR_LANES
        vo = ATTN_KV_DUP + j * ATTN_PAIR_LANES
        kk = jnp.concatenate([kv_prev[:, ko:ko + ATTN_PAIR_LANES], kv_cur[:, ko:ko + ATTN_PAIR_LANES]], axis=0)
        vv = jnp.concatenate([kv_prev[:, vo:vo + ATTN_PAIR_LANES], kv_cur[:, vo:vo + ATTN_PAIR_LANES]], axis=0)
        k_bd = jnp.concatenate([jnp.where(low, kk, zero), jnp.where(low, zero, kk)], axis=0)
        v_bd = jnp.concatenate([jnp.where(low, vv, zero), jnp.where(low, zero, vv)], axis=0)
        qo = j * ATTN_GROUP * ATTN_HEAD_DIM
        q_st = jnp.concatenate(
            [q[:, qo + p * ATTN_PAIR_LANES:qo + (p + 1) * ATTN_PAIR_LANES] for p in range(ATTN_PAIRS_PER_KV)],
            axis=0)
        s = _mm_nt(q_st, k_bd)
        p_rows = []
        inv = []
        for p in range(ATTN_PAIRS_PER_KV):
            tiles = []
            for par in range(2):
                sink = sink_ref[j * ATTN_GROUP + 2 * p + par]
                base = par * 2 * blk
                s_prev = s[p * blk:(p + 1) * blk, base:base + blk]
                s_cur = s[p * blk:(p + 1) * blk, base + blk:base + 2 * blk]
                if first is not None:
                    s_prev = jnp.where(first, NEG_INF, s_prev)
                logit = jnp.where(from_prev, s_prev, s_cur)
                m = jnp.maximum(jnp.max(logit, axis=-1, keepdims=True), sink)
                e = jnp.exp(logit - m)
                denom = jnp.sum(e, axis=-1, keepdims=True) + jnp.exp(sink - m)
                inv.append(1.0 / denom)
                eb = e.astype(BF16)
                tiles += [jnp.where(from_prev, eb, zero), jnp.where(from_prev, zero, eb)]
            p_rows.append(jnp.concatenate(tiles, axis=1))
        o = _mm(jnp.concatenate(p_rows, axis=0), v_bd)
        for p in range(ATTN_PAIRS_PER_KV):
            scale = jnp.where(lane_q, inv[2 * p], inv[2 * p + 1])
            out_tiles.append((o[p * blk:(p + 1) * blk, :] * scale).astype(BF16))
    return jnp.concatenate(out_tiles, axis=1)


def _attn_layer_body(sink_ref, x_ref, w_ref, bqkv_ref, wo_ref, bo_ref, g_ref, b_ref, o_ref, kv_ref):
    blk = ATTN_BLOCK
    tile = x_ref.shape[0]
    at_start = pl.program_id(1) == 0

    @pl.when(at_start)
    def _():
        kv_ref[...] = jnp.zeros_like(kv_ref)

    x = x_ref[...]
    xb = x.astype(BF16)
    q = ((_mm(xb, w_ref[:, :ATTN_Q_DIM]) + bqkv_ref[:, :ATTN_Q_DIM]) * (ATTN_HEAD_DIM ** -0.5)).astype(BF16)
    kv = (_mm(xb, w_ref[:, ATTN_Q_DIM:]) + bqkv_ref[:, ATTN_Q_DIM:]).astype(BF16)
    heads = []
    for r in range(tile // blk):
        rows = slice(r * blk, (r + 1) * blk)
        kv_prev = kv_ref[...].astype(BF16) if r == 0 else kv[(r - 1) * blk:r * blk]
        heads.append(_swa_block(q[rows], kv[rows], kv_prev, sink_ref, at_start if r == 0 else None))
    kv_ref[...] = kv[tile - blk:].astype(F32)
    mix = _mm(jnp.concatenate(heads, axis=0), wo_ref[...]) + bo_ref[...]
    o_ref[...] = _layer_norm(DEEPNORM_ALPHA * x + mix, g_ref[...], b_ref[...])


def _attn_layer(x, w_qkv, b_qkv, sinks, wo, bo, layer, g, b, batch, seq):
    d = x.shape[-1]
    tile = TOKEN_TILE
    x = x.reshape(batch, seq, d)
    block_bytes = 2 * 2 * tile * d * 4 + (w_qkv[0].size + wo[0].size) * 2
    temp_bytes = 12 * tile * d * 4
    out = pl.pallas_call(
        _attn_layer_body,
        grid=(batch, seq // tile),
        in_specs=[
            pl.BlockSpec(memory_space=pltpu.SMEM),
            pl.BlockSpec((None, tile, d), lambda bi, n: (bi, n, 0)),
            _layer_block(w_qkv.shape, layer),
            _resident((1, ATTN_PROJ_DIM)),
            _layer_block(wo.shape, layer),
            _resident((1, d)),
            _resident((1, d)),
            _resident((1, d)),
        ],
        out_specs=pl.BlockSpec((None, tile, d), lambda bi, n: (bi, n, 0)),
        out_shape=jax.ShapeDtypeStruct((batch, seq, d), F32),
        scratch_shapes=[pltpu.VMEM((ATTN_BLOCK, 2 * ATTN_KV_DUP), F32)],
        compiler_params=pltpu.CompilerParams(
            dimension_semantics=("parallel", "arbitrary"),
            vmem_limit_bytes=_vmem_limit(block_bytes, temp_bytes)),
        name="attention_layer",
    )(sinks, x, w_qkv, b_qkv.reshape(1, ATTN_PROJ_DIM), wo, bo.reshape(1, d), g.reshape(1, d), b.reshape(1, d))
    return out.reshape(batch * seq, d)


def _attn_qkv_layout(w_qkv, b_qkv):
    def dup(t, off):
        parts = []
        for j in range(ATTN_KV_HEADS):
            col = t[..., off + j * ATTN_HEAD_DIM:off + (j + 1) * ATTN_HEAD_DIM]
            parts += [col, col]
        return parts
    def relayout(t):
        return jnp.concatenate([t[..., :ATTN_Q_DIM]] + dup(t, ATTN_Q_DIM) + dup(t, ATTN_Q_DIM + ATTN_KV_DIM), axis=-1)
    return relayout(w_qkv), relayout(b_qkv)


def _ret_proj_columns(h):
    return ((h * RET_QK_DIM, RET_QK_DIM),
            (RET_QK_ALL + h * RET_QK_DIM, RET_QK_DIM),
            (2 * RET_QK_ALL + h * RET_V_DIM, RET_V_DIM),
            (2 * RET_QK_ALL + RET_V_ALL + h * RET_V_DIM, RET_V_DIM))


def _ret_layer_body(tiles_per_seq, xc_ref, xn_ref, w_ref, wo_ref, cos_ref, sin_ref, g_ref, b_ref, o_ref,
                    state_ref, decay_ref, *proj_refs):
    c = RET_STEP
    n_chunks = xc_ref.shape[0] // c
    step = pl.program_id(0)

    def project():
        xb = xn_ref[...].astype(BF16)
        for h in range(RET_HEADS):
            for i, (lo, width) in enumerate(_ret_proj_columns(h)):
                proj_refs[4 * h + i][...] = _mm(xb, w_ref[:, lo:lo + width])

    @pl.when(step == 0)
    def _():
        state_ref[...] = jnp.zeros_like(state_ref)
        row = lax.broadcasted_iota(jnp.int32, (c, c), 0).astype(F32)
        col = lax.broadcasted_iota(jnp.int32, (c, c), 1).astype(F32)
        diff = row - col
        for h in range(RET_HEADS):
            decay_ref[h] = jnp.where(diff >= 0, jnp.exp(jnp.maximum(diff, 0.0) * RET_LOG_GAMMA[h]), 0.0)
        project()

    def project_head(h, xb, parts=(0, 1, 2, 3)):
        columns = _ret_proj_columns(h)
        for i in parts:
            lo, width = columns[i]
            proj_refs[4 * h + i][...] = _mm(xb, w_ref[:, lo:lo + width])

    def load_head(h):
        lg = RET_LOG_GAMMA[h]
        idx = lax.broadcasted_iota(jnp.int32, (c, 1), 0).astype(F32)
        q_decay = jnp.exp((idx + 1.0) * lg)
        k_decay = jnp.exp((c - 1.0 - idx) * lg)
        out = []
        for r in range(n_chunks):
            rows = slice(r * c, (r + 1) * c)
            cos = cos_ref[rows, :]
            sin = sin_ref[rows, :]

            def rot(t):
                t1 = t[:, :ROPE_HALF]
                t2 = t[:, ROPE_HALF:]
                return jnp.concatenate([t1 * cos - t2 * sin, t1 * sin + t2 * cos], axis=-1)

            q = rot(proj_refs[4 * h][rows, :])
            k = rot(proj_refs[4 * h + 1][rows, :]) * (RET_QK_DIM ** -0.5)
            v = proj_refs[4 * h + 2][rows, :].astype(BF16)
            gate = proj_refs[4 * h + 3][rows, :]
            out.append((q.astype(BF16), (q * q_decay).astype(BF16), k.astype(BF16), (k * k_decay).astype(BF16), v,
                        gate * jax.nn.sigmoid(gate)))
        return out

    def core(h, chunks, fresh):
        chunk_decay = math.exp(c * RET_LOG_GAMMA[h])
        decay_in = decay_ref[h]
        gated = []
        for r, (qb, qd, kb, kd, v, sg) in enumerate(chunks):
            qk = _mm_nt(qb, kb) * decay_in
            state = state_ref[h]
            if r == 0:
                state = jnp.where(fresh, 0.0, state)
            cross = _mm(qd, state.astype(BF16))
            kv = _mm_tn(kd, v)
            inner = _mm(qk.astype(BF16), v)
            state_ref[h] = state * chunk_decay + kv
            y = inner + cross
            mu = jnp.mean(y, axis=-1, keepdims=True)
            yc = y - mu
            var = jnp.mean(yc * yc, axis=-1, keepdims=True)
            gated.append((sg * (yc * lax.rsqrt(var + GN_EPS))).astype(BF16))
        return jnp.concatenate(gated, axis=0)

    @pl.when(step > 0)
    def _():
        fresh = lax.rem(step - 1, tiles_per_seq) == 0
        x = xc_ref[...]
        xb_next = xn_ref[...].astype(BF16)
        loaded = load_head(0)
        project_head(0, xb_next, (0, 1))
        mix = None
        for h in range(RET_HEADS):
            gated = core(h, loaded, fresh)
            if h + 1 < RET_HEADS:
                loaded = load_head(h + 1)
                project_head(h + 1, xb_next)
            else:
                project_head(0, xb_next, (2,))
            part = _mm(gated, wo_ref[h * RET_V_DIM:(h + 1) * RET_V_DIM, :])
            mix = part if mix is None else mix + part
        project_head(0, xb_next, (3,))
        o_ref[...] = _layer_norm(DEEPNORM_ALPHA * x + mix, g_ref[...], b_ref[...])


def _ret_layer(x, w, wo, layer, cos, sin, g, b, batch, seq):
    t, d = x.shape
    tile = TOKEN_TILE
    n_tiles = t // tile
    tiles_per_seq = seq // tile
    proj_shapes = [pltpu.VMEM((tile, width), F32) for h in range(RET_HEADS) for _, width in _ret_proj_columns(h)]
    block_bytes = (3 * 2 * tile * d * 4 + (w[0].size + wo[0].size) * 2 + 2 * 2 * tile * ROPE_HALF * 4
                   + tile * RET_PROJ_DIM * 4)
    temp_bytes = (RET_HEADS * RET_QK_DIM * RET_V_DIM * 4 + RET_HEADS * RET_STEP * RET_STEP * 4
                  + tile * RET_PROJ_DIM * 4)
    prev_tile = lambda s: jnp.maximum(s - 1, 0)
    return pl.pallas_call(
        functools.partial(_ret_layer_body, tiles_per_seq),
        grid=(n_tiles + 1,),
        in_specs=[
            pl.BlockSpec((tile, d), lambda s: (prev_tile(s), 0)),
            pl.BlockSpec((tile, d), lambda s: (jnp.minimum(s, n_tiles - 1), 0)),
            _layer_block(w.shape, layer),
            _layer_block(wo.shape, layer),
            pl.BlockSpec((tile, ROPE_HALF), lambda s: (lax.rem(prev_tile(s), tiles_per_seq), 0)),
            pl.BlockSpec((tile, ROPE_HALF), lambda s: (lax.rem(prev_tile(s), tiles_per_seq), 0)),
            _resident((1, d)),
            _resident((1, d)),
        ],
        out_specs=pl.BlockSpec((tile, d), lambda s: (prev_tile(s), 0)),
        out_shape=jax.ShapeDtypeStruct((t, d), F32),
        scratch_shapes=[pltpu.VMEM((RET_HEADS, RET_QK_DIM, RET_V_DIM), F32),
                        pltpu.VMEM((RET_HEADS, RET_STEP, RET_STEP), F32)] + proj_shapes,
        compiler_params=pltpu.CompilerParams(
            dimension_semantics=("arbitrary",),
            vmem_limit_bytes=_vmem_limit(block_bytes, temp_bytes)),
        name="retention_layer",
    )(x, x, w, wo, cos, sin, g.reshape(1, d), b.reshape(1, d))


def _rope_tables(seq):
    pos = jnp.arange(seq, dtype=F32)
    freqs = 1.0 / (ROPE_BASE ** jnp.linspace(0.0, 1.0, ROPE_HALF, dtype=F32))
    ang = pos[:, None] * freqs[None, :]
    return jnp.cos(ang), jnp.sin(ang)


def kernel(x, ln_g, ln_b, ffn1_w_gate_up, ffn1_w_down, ffn2_w_gate_up, ffn2_w_down,
           attn_w_qkv, attn_b_qkv, attn_sinks, attn_w_o, attn_b_o, ret_w_qkvg, ret_w_o):
    batch, seq, d = x.shape
    assert d == D_MODEL and seq % TOKEN_TILE == 0 and TOKEN_TILE % ATTN_BLOCK == 0 and TOKEN_TILE % RET_STEP == 0
    cos, sin = _rope_tables(seq)
    w_qkv, b_qkv = _attn_qkv_layout(attn_w_qkv, attn_b_qkv)
    w_qkv = w_qkv.astype(BF16)
    attn_wo = attn_w_o.astype(BF16)
    ret_w = ret_w_qkvg.astype(BF16)
    ret_wo = ret_w_o.astype(BF16)
    h = x.reshape(batch * seq, d)
    for i in range(DEPTH):
        h = _ffn_ln(h, ffn1_w_gate_up, ffn1_w_down, i, ln_g[i, 0], ln_b[i, 0])
        j = i // N_MIXERS
        if i % N_MIXERS == 0:
            h = _attn_layer(h, w_qkv, b_qkv[j], attn_sinks[j], attn_wo, attn_b_o[j], j,
                            ln_g[i, 1], ln_b[i, 1], batch, seq)
        else:
            h = _ret_layer(h, ret_w, ret_wo, j, cos, sin, ln_g[i, 1], ln_b[i, 1], batch, seq)
        h = _ffn_ln(h, ffn2_w_gate_up, ffn2_w_down, i, ln_g[i, 2], ln_b[i, 2])
    return h.reshape(batch, seq, d)
```

```python
import functools
import math

import jax
import jax.numpy as jnp
from jax import lax
from jax.experimental import pallas as pl
from jax.experimental.pallas import tpu as pltpu

D_MODEL = 1024
DEPTH = 4
N_MIXERS = 2
ATTN_Q_HEADS = 16
ATTN_KV_HEADS = 2
ATTN_HEAD_DIM = 64
WINDOW = 128
ATTN_BLOCK = 128
RET_HEADS = 4
RET_QK_DIM = D_MODEL // RET_HEADS
RET_V_DIM = 2 * D_MODEL // RET_HEADS
ROPE_BASE = 10000.0
FFN_DIM = 2816
DEEPNORM_ALPHA = (2.0 * DEPTH) ** 0.25
LN_EPS = 1e-5
GN_EPS = 1e-6
NEG_INF = -1e30

ATTN_GROUP = ATTN_Q_HEADS // ATTN_KV_HEADS
ATTN_Q_DIM = ATTN_Q_HEADS * ATTN_HEAD_DIM
ATTN_KV_DIM = ATTN_KV_HEADS * ATTN_HEAD_DIM
RET_QK_ALL = RET_HEADS * RET_QK_DIM
RET_V_ALL = RET_HEADS * RET_V_DIM
RET_PROJ_DIM = 2 * RET_QK_ALL + 2 * RET_V_ALL
ROPE_HALF = RET_QK_DIM // 2
RET_LOG_GAMMA = tuple(math.log(1.0 - 2.0 ** (-5.0 - h)) for h in range(RET_HEADS))

V7X_LANES = 128
V7X_MXU_DIM = 256
V7X_VMEM_BYTES = 64 * 1024 * 1024

TOKEN_TILE = 512
FFN_CHUNK = V7X_MXU_DIM
RET_STEP = V7X_MXU_DIM

F32 = jnp.float32
BF16 = jnp.bfloat16


def _vmem_limit(block_bytes, temp_bytes):
    want = block_bytes + temp_bytes
    return int(min(want, V7X_VMEM_BYTES - 8 * 1024 * 1024))


def _resident(shape):
    return pl.BlockSpec(shape, lambda *_: (0,) * len(shape), pipeline_mode=pl.Buffered(1))


def _layer_block(shape, layer):
    return pl.BlockSpec((None,) + tuple(shape[1:]), lambda *_: (layer,) + (0,) * (len(shape) - 1),
                        pipeline_mode=pl.Buffered(1))


def _layer_norm(y, g, b):
    mu = jnp.mean(y, axis=-1, keepdims=True)
    yc = y - mu
    var = jnp.mean(yc * yc, axis=-1, keepdims=True)
    return yc * lax.rsqrt(var + LN_EPS) * g + b


def _mm(a, b):
    return jnp.dot(a, b, preferred_element_type=F32)


def _mm_nt(a, b):
    return lax.dot_general(a, b, (((1,), (1,)), ((), ())), preferred_element_type=F32)


def _mm_tn(a, b):
    return lax.dot_general(a, b, (((0,), (0,)), ((), ())), preferred_element_type=F32)


def _ffn_ln_body(x_ref, wgu_ref, wd_ref, g_ref, b_ref, o_ref):
    x = x_ref[...]
    xb = x.astype(BF16)
    acc = None
    for c in range(FFN_DIM // FFN_CHUNK):
        lo = c * FFN_CHUNK
        gate = _mm(xb, wgu_ref[:, lo:lo + FFN_CHUNK].astype(BF16))
        up = _mm(xb, wgu_ref[:, FFN_DIM + lo:FFN_DIM + lo + FFN_CHUNK].astype(BF16))
        h = (gate * jax.nn.sigmoid(gate) * up).astype(BF16)
        part = _mm(h, wd_ref[lo:lo + FFN_CHUNK, :].astype(BF16))
        acc = part if acc is None else acc + part
    y = DEEPNORM_ALPHA * x + 0.5 * acc
    o_ref[...] = _layer_norm(y, g_ref[...], b_ref[...])


def _ffn_ln(x, w_gate_up, w_down, layer, g, b):
    t, d = x.shape
    tm = TOKEN_TILE
    block_bytes = 2 * 2 * tm * d * 4 + (w_gate_up[0].size + w_down[0].size) * 4
    temp_bytes = 6 * tm * d * 4
    return pl.pallas_call(
        _ffn_ln_body,
        grid=(t // tm,),
        in_specs=[
            pl.BlockSpec((tm, d), lambda i: (i, 0)),
            _layer_block(w_gate_up.shape, layer),
            _layer_block(w_down.shape, layer),
            _resident((1, d)),
            _resident((1, d)),
        ],
        out_specs=pl.BlockSpec((tm, d), lambda i: (i, 0)),
        out_shape=jax.ShapeDtypeStruct((t, d), F32),
        compiler_params=pltpu.CompilerParams(
            dimension_semantics=("parallel",),
            vmem_limit_bytes=_vmem_limit(block_bytes, temp_bytes)),
        name="ffn_ln",
    )(x, w_gate_up, w_down, g.reshape(1, d), b.reshape(1, d))


ATTN_PAIR_LANES = 2 * ATTN_HEAD_DIM
ATTN_PAIRS_PER_KV = ATTN_GROUP // 2
ATTN_KV_DUP = ATTN_KV_HEADS * ATTN_PAIR_LANES
ATTN_PROJ_DIM = ATTN_Q_DIM + 2 * ATTN_KV_DUP
assert WINDOW == ATTN_BLOCK and ATTN_PAIR_LANES == V7X_LANES and ATTN_GROUP % 2 == 0


def _swa_scores(q, kv_cur, kv_prev, j):
    blk = ATTN_BLOCK
    low = lax.broadcasted_iota(jnp.int32, (2 * blk, ATTN_PAIR_LANES), 1) < ATTN_HEAD_DIM
    zero = jnp.zeros((), BF16)
    ko = j * ATTN_PAIR_LANES
    vo = ATTN_KV_DUP + j * ATTN_PAIR_LANES
    kk = jnp.concatenate([kv_prev[:, ko:ko + ATTN_PAIR_LANES], kv_cur[:, ko:ko + ATTN_PAIR_LANES]], axis=0)
    vv = jnp.concatenate([kv_prev[:, vo:vo + ATTN_PAIR_LANES], kv_cur[:, vo:vo + ATTN_PAIR_LANES]], axis=0)
    k_bd = jnp.concatenate([jnp.where(low, kk, zero), jnp.where(low, zero, kk)], axis=0)
    v_bd = jnp.concatenate([jnp.where(low, vv, zero), jnp.where(low, zero, vv)], axis=0)
    qo = j * ATTN_GROUP * ATTN_HEAD_DIM
    q_st = jnp.concatenate(
        [q[:, qo + p * ATTN_PAIR_LANES:qo + (p + 1) * ATTN_PAIR_LANES] for p in range(ATTN_PAIRS_PER_KV)],
        axis=0)
    return _mm_nt(q_st, k_bd), v_bd


def _swa_finish(s, v_bd, sink_ref, first, j):
    blk = ATTN_BLOCK
    qi = lax.broadcasted_iota(jnp.int32, (blk, blk), 0)
    kc = lax.broadcasted_iota(jnp.int32, (blk, blk), 1)
    from_prev = kc > qi
    lane_q = lax.broadcasted_iota(jnp.int32, (blk, ATTN_PAIR_LANES), 1) < ATTN_HEAD_DIM
    zero = jnp.zeros((), BF16)
    p_rows = []
    inv = []
    for p in range(ATTN_PAIRS_PER_KV):
        tiles = []
        for par in range(2):
            sink = sink_ref[j * ATTN_GROUP + 2 * p + par]
            base = par * 2 * blk
            s_prev = s[p * blk:(p + 1) * blk, base:base + blk]
            s_cur = s[p * blk:(p + 1) * blk, base + blk:base + 2 * blk]
            if first is not None:
                s_prev = jnp.where(first, NEG_INF, s_prev)
            logit = jnp.where(from_prev, s_prev, s_cur)
            m = jnp.maximum(jnp.max(logit, axis=-1, keepdims=True), sink)
            e = jnp.exp(logit - m)
            denom = jnp.sum(e, axis=-1, keepdims=True) + jnp.exp(sink - m)
            inv.append(1.0 / denom)
            eb = e.astype(BF16)
            tiles += [jnp.where(from_prev, eb, zero), jnp.where(from_prev, zero, eb)]
        p_rows.append(jnp.concatenate(tiles, axis=1))
    o = _mm(jnp.concatenate(p_rows, axis=0), v_bd)
    out_tiles = []
    for p in range(ATTN_PAIRS_PER_KV):
        scale = jnp.where(lane_q, inv[2 * p], inv[2 * p + 1])
        out_tiles.append((o[p * blk:(p + 1) * blk, :] * scale).astype(BF16))
    return out_tiles


ATTN_PROJ_PIECE = V7X_MXU_DIM
ATTN_PROJ_PIECES = ATTN_PROJ_DIM // ATTN_PROJ_PIECE
ATTN_TAIL_PIECES = 2


def _attn_layer_body(tiles_per_seq, sink_ref, xc_ref, xn_ref, w_ref, bqkv_ref, wo_ref, bo_ref, g_ref, b_ref, o_ref,
                     kv_ref, *proj_refs):
    blk = ATTN_BLOCK
    tile = xc_ref.shape[0]
    step = pl.program_id(0)

    def project_piece(p, xb):
        lo = p * ATTN_PROJ_PIECE
        proj_refs[p][...] = _mm(xb, w_ref[:, lo:lo + ATTN_PROJ_PIECE])

    @pl.when(step == 0)
    def _():
        kv_ref[...] = jnp.zeros_like(kv_ref)
        xb = xn_ref[...].astype(BF16)
        for p in range(ATTN_PROJ_PIECES):
            project_piece(p, xb)

    @pl.when(step > 0)
    def _():
        at_start = lax.rem(step - 1, tiles_per_seq) == 0
        x = xc_ref[...]
        xb_next = xn_ref[...].astype(BF16)
        proj = jnp.concatenate([proj_refs[p][...] for p in range(ATTN_PROJ_PIECES)], axis=1) + bqkv_ref[...]
        q = (proj[:, :ATTN_Q_DIM] * (ATTN_HEAD_DIM ** -0.5)).astype(BF16)
        kv = proj[:, ATTN_Q_DIM:].astype(BF16)
        units = [(r, j) for r in range(tile // blk) for j in range(ATTN_KV_HEADS)]
        pieces = iter(range(ATTN_PROJ_PIECES - ATTN_TAIL_PIECES))

        def scores(u):
            r, j = units[u]
            rows = slice(r * blk, (r + 1) * blk)
            kv_prev = kv_ref[...].astype(BF16) if r == 0 else kv[(r - 1) * blk:r * blk]
            return _swa_scores(q[rows], kv[rows], kv_prev, j)

        out_tiles = {}
        pending = scores(0)
        for u, (r, j) in enumerate(units):
            s, v_bd = pending
            if u + 1 < len(units):
                pending = scores(u + 1)
            p = next(pieces, None)
            if p is not None:
                project_piece(p, xb_next)
            out_tiles[(r, j)] = _swa_finish(s, v_bd, sink_ref, at_start if r == 0 else None, j)
        assert next(pieces, None) is None
        kv_ref[...] = kv[tile - blk:].astype(F32)
        heads = jnp.concatenate(
            [jnp.concatenate([t for j in range(ATTN_KV_HEADS) for t in out_tiles[(r, j)]], axis=1)
             for r in range(tile // blk)], axis=0)
        mix = _mm(heads, wo_ref[...]) + bo_ref[...]
        for p in range(ATTN_PROJ_PIECES - ATTN_TAIL_PIECES, ATTN_PROJ_PIECES):
            project_piece(p, xb_next)
        o_ref[...] = _layer_norm(DEEPNORM_ALPHA * x + mix, g_ref[...], b_ref[...])


def _attn_layer(x, w_qkv, b_qkv, sinks, wo, bo, layer, g, b, batch, seq):
    t, d = x.shape
    tile = TOKEN_TILE
    n_tiles = t // tile
    tiles_per_seq = seq // tile
    block_bytes = 3 * 2 * tile * d * 4 + (w_qkv[0].size + wo[0].size) * 2 + tile * ATTN_PROJ_DIM * 4
    temp_bytes = 12 * tile * d * 4
    prev_tile = lambda s: jnp.maximum(s - 1, 0)
    return pl.pallas_call(
        functools.partial(_attn_layer_body, tiles_per_seq),
        grid=(n_tiles + 1,),
        in_specs=[
            pl.BlockSpec(memory_space=pltpu.SMEM),
            pl.BlockSpec((tile, d), lambda s: (prev_tile(s), 0)),
            pl.BlockSpec((tile, d), lambda s: (jnp.minimum(s, n_tiles - 1), 0)),
            _layer_block(w_qkv.shape, layer),
            _resident((1, ATTN_PROJ_DIM)),
            _layer_block(wo.shape, layer),
            _resident((1, d)),
            _resident((1, d)),
            _resident((1, d)),
        ],
        out_specs=pl.BlockSpec((tile, d), lambda s: (prev_tile(s), 0)),
        out_shape=jax.ShapeDtypeStruct((t, d), F32),
        scratch_shapes=[pltpu.VMEM((ATTN_BLOCK, 2 * ATTN_KV_DUP), F32)]
        + [pltpu.VMEM((tile, ATTN_PROJ_PIECE), F32) for _ in range(ATTN_PROJ_PIECES)],
        compiler_params=pltpu.CompilerParams(
            dimension_semantics=("arbitrary",),
            vmem_limit_bytes=_vmem_limit(block_bytes, temp_bytes)),
        name="attention_layer",
    )(sinks, x, x, w_qkv, b_qkv.reshape(1, ATTN_PROJ_DIM), wo, bo.reshape(1, d), g.reshape(1, d), b.reshape(1, d))


def _attn_qkv_layout(w_qkv, b_qkv):
    def dup(t, off):
        parts = []
        for j in range(ATTN_KV_HEADS):
            col = t[..., off + j * ATTN_HEAD_DIM:off + (j + 1) * ATTN_HEAD_DIM]
            parts += [col, col]
        return parts
    def relayout(t):
        return jnp.concatenate([t[..., :ATTN_Q_DIM]] + dup(t, ATTN_Q_DIM) + dup(t, ATTN_Q_DIM + ATTN_KV_DIM), axis=-1)
    return relayout(w_qkv), relayout(b_qkv)


def _ret_proj_columns(h):
    return ((h * RET_QK_DIM, RET_QK_DIM),
            (RET_QK_ALL + h * RET_QK_DIM, RET_QK_DIM),
            (2 * RET_QK_ALL + h * RET_V_DIM, RET_V_DIM),
            (2 * RET_QK_ALL + RET_V_ALL + h * RET_V_DIM, RET_V_DIM))


def _ret_layer_body(tiles_per_seq, xc_ref, xn_ref, w_ref, wo_ref, cos_ref, sin_ref, g_ref, b_ref, o_ref,
                    state_ref, decay_ref, *proj_refs):
    c = RET_STEP
    n_chunks = xc_ref.shape[0] // c
    step = pl.program_id(0)

    def project():
        xb = xn_ref[...].astype(BF16)
        for h in range(RET_HEADS):
            for i, (lo, width) in enumerate(_ret_proj_columns(h)):
                proj_refs[4 * h + i][...] = _mm(xb, w_ref[:, lo:lo + width])

    @pl.when(step == 0)
    def _():
        state_ref[...] = jnp.zeros_like(state_ref)
        row = lax.broadcasted_iota(jnp.int32, (c, c), 0).astype(F32)
        col = lax.broadcasted_iota(jnp.int32, (c, c), 1).astype(F32)
        diff = row - col
        for h in range(RET_HEADS):
            decay_ref[h] = jnp.where(diff >= 0, jnp.exp(jnp.maximum(diff, 0.0) * RET_LOG_GAMMA[h]), 0.0)
        project()

    def project_head(h, xb, parts=(0, 1, 2, 3)):
        columns = _ret_proj_columns(h)
        for i in parts:
            lo, width = columns[i]
            proj_refs[4 * h + i][...] = _mm(xb, w_ref[:, lo:lo + width])

    def load_head(h):
        lg = RET_LOG_GAMMA[h]
        idx = lax.broadcasted_iota(jnp.int32, (c, 1), 0).astype(F32)
        q_decay = jnp.exp((idx + 1.0) * lg)
        k_decay = jnp.exp((c - 1.0 - idx) * lg)
        out = []
        for r in range(n_chunks):
            rows = slice(r * c, (r + 1) * c)
            cos = cos_ref[rows, :]
            sin = sin_ref[rows, :]

            def rot(t):
                t1 = t[:, :ROPE_HALF]
                t2 = t[:, ROPE_HALF:]
                return jnp.concatenate([t1 * cos - t2 * sin, t1 * sin + t2 * cos], axis=-1)

            q = rot(proj_refs[4 * h][rows, :])
            k = rot(proj_refs[4 * h + 1][rows, :]) * (RET_QK_DIM ** -0.5)
            v = proj_refs[4 * h + 2][rows, :].astype(BF16)
            gate = proj_refs[4 * h + 3][rows, :]
            out.append((q.astype(BF16), (q * q_decay).astype(BF16), k.astype(BF16), (k * k_decay).astype(BF16), v,
                        gate * jax.nn.sigmoid(gate)))
        return out

    def core(h, chunks, fresh):
        chunk_decay = math.exp(c * RET_LOG_GAMMA[h])
        decay_in = decay_ref[h]
        gated = []
        for r, (qb, qd, kb, kd, v, sg) in enumerate(chunks):
            qk = _mm_nt(qb, kb) * decay_in
            state = state_ref[h]
            if r == 0:
                state = jnp.where(fresh, 0.0, state)
            cross = _mm(qd, state.astype(BF16))
            kv = _mm_tn(kd, v)
            inner = _mm(qk.astype(BF16), v)
            state_ref[h] = state * chunk_decay + kv
            y = inner + cross
            mu = jnp.mean(y, axis=-1, keepdims=True)
            yc = y - mu
            var = jnp.mean(yc * yc, axis=-1, keepdims=True)
            gated.append((sg * (yc * lax.rsqrt(var + GN_EPS))).astype(BF16))
        return jnp.concatenate(gated, axis=0)

    @pl.when(step > 0)
    def _():
        fresh = lax.rem(step - 1, tiles_per_seq) == 0
        x = xc_ref[...]
        xb_next = xn_ref[...].astype(BF16)
        loaded = load_head(0)
        project_head(0, xb_next, (0, 1))
        mix = None
        for h in range(RET_HEADS):
            gated = core(h, loaded, fresh)
            if h + 1 < RET_HEADS:
                loaded = load_head(h + 1)
                project_head(h + 1, xb_next)
            else:
                project_head(0, xb_next, (2,))
            part = _mm(gated, wo_ref[h * RET_V_DIM:(h + 1) * RET_V_DIM, :])
            mix = part if mix is None else mix + part
        project_head(0, xb_next, (3,))
        o_ref[...] = _layer_norm(DEEPNORM_ALPHA * x + mix, g_ref[...], b_ref[...])


def _ret_layer(x, w, wo, layer, cos, sin, g, b, batch, seq):
    t, d = x.shape
    tile = TOKEN_TILE
    n_tiles = t // tile
    tiles_per_seq = seq // tile
    proj_shapes = [pltpu.VMEM((tile, width), F32) for h in range(RET_HEADS) for _, width in _ret_proj_columns(h)]
    block_bytes = (3 * 2 * tile * d * 4 + (w[0].size + wo[0].size) * 2 + 2 * 2 * tile * ROPE_HALF * 4
                   + tile * RET_PROJ_DIM * 4)
    temp_bytes = (RET_HEADS * RET_QK_DIM * RET_V_DIM * 4 + RET_HEADS * RET_STEP * RET_STEP * 4
                  + tile * RET_PROJ_DIM * 4)
    prev_tile = lambda s: jnp.maximum(s - 1, 0)
    return pl.pallas_call(
        functools.partial(_ret_layer_body, tiles_per_seq),
        grid=(n_tiles + 1,),
        in_specs=[
            pl.BlockSpec((tile, d), lambda s: (prev_tile(s), 0)),
            pl.BlockSpec((tile, d), lambda s: (jnp.minimum(s, n_tiles - 1), 0)),
            _layer_block(w.shape, layer),
            _layer_block(wo.shape, layer),
            pl.BlockSpec((tile, ROPE_HALF), lambda s: (lax.rem(prev_tile(s), tiles_per_seq), 0)),
            pl.BlockSpec((tile, ROPE_HALF), lambda s: (lax.rem(prev_tile(s), tiles_per_seq), 0)),
            _resident((1, d)),
            _resident((1, d)),
        ],
        out_specs=pl.BlockSpec((tile, d), lambda s: (prev_tile(s), 0)),
        out_shape=jax.ShapeDtypeStruct((t, d), F32),
        scratch_shapes=[pltpu.VMEM((RET_HEADS, RET_QK_DIM, RET_V_DIM), F32),
                        pltpu.VMEM((RET_HEADS, RET_STEP, RET_STEP), F32)] + proj_shapes,
        compiler_params=pltpu.CompilerParams(
            dimension_semantics=("arbitrary",),
            vmem_limit_bytes=_vmem_limit(block_bytes, temp_bytes)),
        name="retention_layer",
    )(x, x, w, wo, cos, sin, g.reshape(1, d), b.reshape(1, d))


def _rope_tables(seq):
    pos = jnp.arange(seq, dtype=F32)
    freqs = 1.0 / (ROPE_BASE ** jnp.linspace(0.0, 1.0, ROPE_HALF, dtype=F32))
    ang = pos[:, None] * freqs[None, :]
    return jnp.cos(ang), jnp.sin(ang)


def kernel(x, ln_g, ln_b, ffn1_w_gate_up, ffn1_w_down, ffn2_w_gate_up, ffn2_w_down,
           attn_w_qkv, attn_b_qkv, attn_sinks, attn_w_o, attn_b_o, ret_w_qkvg, ret_w_o):
    batch, seq, d = x.shape
    assert d == D_MODEL and seq % TOKEN_TILE == 0 and TOKEN_TILE % ATTN_BLOCK == 0 and TOKEN_TILE % RET_STEP == 0
    cos, sin = _rope_tables(seq)
    w_qkv, b_qkv = _attn_qkv_layout(attn_w_qkv, attn_b_qkv)
    w_qkv = w_qkv.astype(BF16)
    attn_wo = attn_w_o.astype(BF16)
    ret_w = ret_w_qkvg.astype(BF16)
    ret_wo = ret_w_o.astype(BF16)
    h = x.reshape(batch * seq, d)
    for i in range(DEPTH):
        h = _ffn_ln(h, ffn1_w_gate_up, ffn1_w_down, i, ln_g[i, 0], ln_b[i, 0])
        j = i // N_MIXERS
        if i % N_MIXERS == 0:
            h = _attn_layer(h, w_qkv, b_qkv[j], attn_sinks[j], attn_wo, attn_b_o[j], j,
                            ln_g[i, 1], ln_b[i, 1], batch, seq)
        else:
            h = _ret_layer(h, ret_w, ret_wo, j, cos, sin, ln_g[i, 1], ln_b[i, 1], batch, seq)
        h = _ffn_ln(h, ffn2_w_gate_up, ffn2_w_down, i, ln_g[i, 2], ln_b[i, 2])
    return h.reshape(batch, seq, d)
```

```python
import functools
import math

import jax
import jax.numpy as jnp
from jax import lax
from jax.experimental import pallas as pl
from jax.experimental.pallas import tpu as pltpu

D_MODEL = 1024
DEPTH = 4
N_MIXERS = 2
ATTN_Q_HEADS = 16
ATTN_KV_HEADS = 2
ATTN_HEAD_DIM = 64
WINDOW = 128
ATTN_BLOCK = 128
RET_HEADS = 4
RET_QK_DIM = D_MODEL // RET_HEADS
RET_V_DIM = 2 * D_MODEL // RET_HEADS
ROPE_BASE = 10000.0
FFN_DIM = 2816
DEEPNORM_ALPHA = (2.0 * DEPTH) ** 0.25
LN_EPS = 1e-5
GN_EPS = 1e-6
NEG_INF = -1e30

ATTN_GROUP = ATTN_Q_HEADS // ATTN_KV_HEADS
ATTN_Q_DIM = ATTN_Q_HEADS * ATTN_HEAD_DIM
ATTN_KV_DIM = ATTN_KV_HEADS * ATTN_HEAD_DIM
RET_QK_ALL = RET_HEADS * RET_QK_DIM
RET_V_ALL = RET_HEADS * RET_V_DIM
RET_PROJ_DIM = 2 * RET_QK_ALL + 2 * RET_V_ALL
ROPE_HALF = RET_QK_DIM // 2
RET_LOG_GAMMA = tuple(math.log(1.0 - 2.0 ** (-5.0 - h)) for h in range(RET_HEADS))

V7X_LANES = 128
V7X_MXU_DIM = 256
V7X_VMEM_BYTES = 64 * 1024 * 1024

TOKEN_TILE = 512
FFN_CHUNK = V7X_MXU_DIM
RET_STEP = V7X_MXU_DIM

F32 = jnp.float32
BF16 = jnp.bfloat16


def _vmem_limit(block_bytes, temp_bytes):
    want = block_bytes + temp_bytes
    return int(min(want, V7X_VMEM_BYTES - 8 * 1024 * 1024))


def _resident(shape):
    return pl.BlockSpec(shape, lambda *_: (0,) * len(shape), pipeline_mode=pl.Buffered(1))


def _layer_block(shape, layer):
    return pl.BlockSpec((None,) + tuple(shape[1:]), lambda *_: (layer,) + (0,) * (len(shape) - 1),
                        pipeline_mode=pl.Buffered(1))


def _layer_norm(y, g, b):
    mu = jnp.mean(y, axis=-1, keepdims=True)
    yc = y - mu
    var = jnp.mean(yc * yc, axis=-1, keepdims=True)
    return yc * lax.rsqrt(var + LN_EPS) * g + b


def _mm(a, b):
    return jnp.dot(a, b, preferred_element_type=F32)


def _mm_nt(a, b):
    return lax.dot_general(a, b, (((1,), (1,)), ((), ())), preferred_element_type=F32)


def _mm_tn(a, b):
    return lax.dot_general(a, b, (((0,), (0,)), ((), ())), preferred_element_type=F32)


FFN_N_CHUNKS = FFN_DIM // FFN_CHUNK


def _ffn_weight_copies(layer, c, wgu_hbm, wd_hbm, wgu_ref, wd_ref, sem):
    lo = c * FFN_CHUNK
    return (
        pltpu.make_async_copy(wgu_hbm.at[layer, :, pl.ds(lo, FFN_CHUNK)],
                              wgu_ref.at[:, pl.ds(lo, FFN_CHUNK)], sem.at[0, c]),
        pltpu.make_async_copy(wgu_hbm.at[layer, :, pl.ds(FFN_DIM + lo, FFN_CHUNK)],
                              wgu_ref.at[:, pl.ds(FFN_DIM + lo, FFN_CHUNK)], sem.at[1, c]),
        pltpu.make_async_copy(wd_hbm.at[layer, pl.ds(lo, FFN_CHUNK), :],
                              wd_ref.at[pl.ds(lo, FFN_CHUNK), :], sem.at[2, c]),
    )


def _ffn_ln_body(layer, x_ref, wgu_hbm, wd_hbm, g_ref, b_ref, o_ref, wgu_ref, wd_ref, sem):
    step = pl.program_id(0)

    def copies(c):
        return _ffn_weight_copies(layer, c, wgu_hbm, wd_hbm, wgu_ref, wd_ref, sem)

    def tile(wait_for_weights):
        x = x_ref[...]
        xb = x.astype(BF16)
        acc = None
        for c in range(FFN_N_CHUNKS):
            lo = c * FFN_CHUNK
            if wait_for_weights:
                for cp in copies(c):
                    cp.wait()
            gate = _mm(xb, wgu_ref[:, lo:lo + FFN_CHUNK].astype(BF16))
            up = _mm(xb, wgu_ref[:, FFN_DIM + lo:FFN_DIM + lo + FFN_CHUNK].astype(BF16))
            h = (gate * jax.nn.sigmoid(gate) * up).astype(BF16)
            part = _mm(h, wd_ref[lo:lo + FFN_CHUNK, :].astype(BF16))
            acc = part if acc is None else acc + part
        y = DEEPNORM_ALPHA * x + 0.5 * acc
        o_ref[...] = _layer_norm(y, g_ref[...], b_ref[...])

    @pl.when(step == 0)
    def _():
        for c in range(FFN_N_CHUNKS):
            for cp in copies(c):
                cp.start()
        tile(wait_for_weights=True)

    @pl.when(step > 0)
    def _():
        tile(wait_for_weights=False)


def _ffn_ln(x, w_gate_up, w_down, layer, g, b):
    t, d = x.shape
    tm = TOKEN_TILE
    block_bytes = 2 * 2 * tm * d * 4 + (w_gate_up[0].size + w_down[0].size) * 4
    temp_bytes = 6 * tm * d * 4
    return pl.pallas_call(
        functools.partial(_ffn_ln_body, layer),
        grid=(t // tm,),
        in_specs=[
            pl.BlockSpec((tm, d), lambda i: (i, 0)),
            pl.BlockSpec(memory_space=pl.ANY),
            pl.BlockSpec(memory_space=pl.ANY),
            _resident((1, d)),
            _resident((1, d)),
        ],
        out_specs=pl.BlockSpec((tm, d), lambda i: (i, 0)),
        out_shape=jax.ShapeDtypeStruct((t, d), F32),
        scratch_shapes=[pltpu.VMEM(w_gate_up.shape[1:], F32), pltpu.VMEM(w_down.shape[1:], F32),
                        pltpu.SemaphoreType.DMA((3, FFN_N_CHUNKS))],
        compiler_params=pltpu.CompilerParams(
            dimension_semantics=("arbitrary",),
            vmem_limit_bytes=_vmem_limit(block_bytes, temp_bytes)),
        name="ffn_ln",
    )(x, w_gate_up, w_down, g.reshape(1, d), b.reshape(1, d))


ATTN_PAIR_LANES = 2 * ATTN_HEAD_DIM
ATTN_PAIRS_PER_KV = ATTN_GROUP // 2
ATTN_KV_DUP = ATTN_KV_HEADS * ATTN_PAIR_LANES
ATTN_PROJ_DIM = ATTN_Q_DIM + 2 * ATTN_KV_DUP
assert WINDOW == ATTN_BLOCK and ATTN_PAIR_LANES == V7X_LANES and ATTN_GROUP % 2 == 0


def _swa_block(q, kv_cur, kv_prev, sink_ref, first):
    blk = ATTN_BLOCK
    lane = lax.broadcasted_iota(jnp.int32, (2 * blk, ATTN_PAIR_LANES), 1)
    low = lane < ATTN_HEAD_DIM
    qi = lax.broadcasted_iota(jnp.int32, (blk, blk), 0)
    kc = lax.broadcasted_iota(jnp.int32, (blk, blk), 1)
    from_prev = kc > qi
    lane_q = lax.broadcasted_iota(jnp.int32, (blk, ATTN_PAIR_LANES), 1) < ATTN_HEAD_DIM
    zero = jnp.zeros((), BF16)
    out_tiles = []
    for j in range(ATTN_KV_HEADS):
        ko = j * ATTN_PAIR_LANES
        vo = ATTN_KV_DUP + j * ATTN_PAIR_LANES
        kk = jnp.concatenate([kv_prev[:, ko:ko + ATTN_PAIR_LANES], kv_cur[:, ko:ko + ATTN_PAIR_LANES]], axis=0)
        vv = jnp.concatenate([kv_prev[:, vo:vo + ATTN_PAIR_LANES], kv_cur[:, vo:vo + ATTN_PAIR_LANES]], axis=0)
        k_bd = jnp.concatenate([jnp.where(low, kk, zero), jnp.where(low, zero, kk)], axis=0)
        v_bd = jnp.concatenate([jnp.where(low, vv, zero), jnp.where(low, zero, vv)], axis=0)
        qo = j * ATTN_GROUP * ATTN_HEAD_DIM
        q_st = jnp.concatenate(
            [q[:, qo + p * ATTN_PAIR_LANES:qo + (p + 1) * ATTN_PAIR_LANES] for p in range(ATTN_PAIRS_PER_KV)],
            axis=0)
        s = _mm_nt(q_st, k_bd)
        p_rows = []
        inv = []
        for p in range(ATTN_PAIRS_PER_KV):
            tiles = []
            for par in range(2):
                sink = sink_ref[j * ATTN_GROUP + 2 * p + par]
                base = par * 2 * blk
                s_prev = s[p * blk:(p + 1) * blk, base:base + blk]
                s_cur = s[p * blk:(p + 1) * blk, base + blk:base + 2 * blk]
                if first is not None:
                    s_prev = jnp.where(first, NEG_INF, s_prev)
                logit = jnp.where(from_prev, s_prev, s_cur)
                m = jnp.maximum(jnp.max(logit, axis=-1, keepdims=True), sink)
                e = jnp.exp(logit - m)
                denom = jnp.sum(e, axis=-1, keepdims=True) + jnp.exp(sink - m)
                inv.append(1.0 / denom)
                eb = e.astype(BF16)
                tiles += [jnp.where(from_prev, eb, zero), jnp.where(from_prev, zero, eb)]
            p_rows.append(jnp.concatenate(tiles, axis=1))
        o = _mm(jnp.concatenate(p_rows, axis=0), v_bd)
        for p in range(ATTN_PAIRS_PER_KV):
            scale = jnp.where(lane_q, inv[2 * p], inv[2 * p + 1])
            out_tiles.append((o[p * blk:(p + 1) * blk, :] * scale).astype(BF16))
    return jnp.concatenate(out_tiles, axis=1)


def _attn_layer_body(sink_ref, x_ref, w_ref, bqkv_ref, wo_ref, bo_ref, g_ref, b_ref, o_ref, kv_ref):
    blk = ATTN_BLOCK
    tile = x_ref.shape[0]
    at_start = pl.program_id(1) == 0

    @pl.when(at_start)
    def _():
        kv_ref[...] = jnp.zeros_like(kv_ref)

    x = x_ref[...]
    xb = x.astype(BF16)
    q = ((_mm(xb, w_ref[:, :ATTN_Q_DIM]) + bqkv_ref[:, :ATTN_Q_DIM]) * (ATTN_HEAD_DIM ** -0.5)).astype(BF16)
    kv = (_mm(xb, w_ref[:, ATTN_Q_DIM:]) + bqkv_ref[:, ATTN_Q_DIM:]).astype(BF16)
    heads = []
    for r in range(tile // blk):
        rows = slice(r * blk, (r + 1) * blk)
        kv_prev = kv_ref[...].astype(BF16) if r == 0 else kv[(r - 1) * blk:r * blk]
        heads.append(_swa_block(q[rows], kv[rows], kv_prev, sink_ref, at_start if r == 0 else None))
    kv_ref[...] = kv[tile - blk:].astype(F32)
    mix = _mm(jnp.concatenate(heads, axis=0), wo_ref[...]) + bo_ref[...]
    o_ref[...] = _layer_norm(DEEPNORM_ALPHA * x + mix, g_ref[...], b_ref[...])


def _attn_layer(x, w_qkv, b_qkv, sinks, wo, bo, layer, g, b, batch, seq):
    d = x.shape[-1]
    tile = TOKEN_TILE
    x = x.reshape(batch, seq, d)
    block_bytes = 2 * 2 * tile * d * 4 + (w_qkv[0].size + wo[0].size) * 2
    temp_bytes = 12 * tile * d * 4
    out = pl.pallas_call(
        _attn_layer_body,
        grid=(batch, seq // tile),
        in_specs=[
            pl.BlockSpec(memory_space=pltpu.SMEM),
            pl.BlockSpec((None, tile, d), lambda bi, n: (bi, n, 0)),
            _layer_block(w_qkv.shape, layer),
            _resident((1, ATTN_PROJ_DIM)),
            _layer_block(wo.shape, layer),
            _resident((1, d)),
            _resident((1, d)),
            _resident((1, d)),
        ],
        out_specs=pl.BlockSpec((None, tile, d), lambda bi, n: (bi, n, 0)),
        out_shape=jax.ShapeDtypeStruct((batch, seq, d), F32),
        scratch_shapes=[pltpu.VMEM((ATTN_BLOCK, 2 * ATTN_KV_DUP), F32)],
        compiler_params=pltpu.CompilerParams(
            dimension_semantics=("parallel", "arbitrary"),
            vmem_limit_bytes=_vmem_limit(block_bytes, temp_bytes)),
        name="attention_layer",
    )(sinks, x, w_qkv, b_qkv.reshape(1, ATTN_PROJ_DIM), wo, bo.reshape(1, d), g.reshape(1, d), b.reshape(1, d))
    return out.reshape(batch * seq, d)


def _attn_qkv_layout(w_qkv, b_qkv):
    def dup(t, off):
        parts = []
        for j in range(ATTN_KV_HEADS):
            col = t[..., off + j * ATTN_HEAD_DIM:off + (j + 1) * ATTN_HEAD_DIM]
            parts += [col, col]
        return parts
    def relayout(t):
        return jnp.concatenate([t[..., :ATTN_Q_DIM]] + dup(t, ATTN_Q_DIM) + dup(t, ATTN_Q_DIM + ATTN_KV_DIM), axis=-1)
    return relayout(w_qkv), relayout(b_qkv)


def _ret_proj_columns(h):
    return ((h * RET_QK_DIM, RET_QK_DIM),
            (RET_QK_ALL + h * RET_QK_DIM, RET_QK_DIM),
            (2 * RET_QK_ALL + h * RET_V_DIM, RET_V_DIM),
            (2 * RET_QK_ALL + RET_V_ALL + h * RET_V_DIM, RET_V_DIM))


def _ret_layer_body(tiles_per_seq, xc_ref, xn_ref, w_ref, wo_ref, cos_ref, sin_ref, g_ref, b_ref, o_ref,
                    state_ref, decay_ref, *proj_refs):
    c = RET_STEP
    n_chunks = xc_ref.shape[0] // c
    step = pl.program_id(0)

    def project_head(h, xb, parts=(0, 1, 2, 3)):
        columns = _ret_proj_columns(h)
        for i in parts:
            lo, width = columns[i]
            proj_refs[4 * h + i][...] = _mm(xb, w_ref[:, lo:lo + width])

    @pl.when(step == 0)
    def _():
        state_ref[...] = jnp.zeros_like(state_ref)
        row = lax.broadcasted_iota(jnp.int32, (c, c), 0).astype(F32)
        col = lax.broadcasted_iota(jnp.int32, (c, c), 1).astype(F32)
        diff = row - col
        for h in range(RET_HEADS):
            decay_ref[h] = jnp.where(diff >= 0, jnp.exp(jnp.maximum(diff, 0.0) * RET_LOG_GAMMA[h]), 0.0)
        xb = xn_ref[...].astype(BF16)
        for h in range(RET_HEADS):
            project_head(h, xb)

    def load_head(h):
        lg = RET_LOG_GAMMA[h]
        idx = lax.broadcasted_iota(jnp.int32, (c, 1), 0).astype(F32)
        q_decay = jnp.exp((idx + 1.0) * lg)
        k_decay = jnp.exp((c - 1.0 - idx) * lg)
        out = []
        for r in range(n_chunks):
            rows = slice(r * c, (r + 1) * c)
            cos = cos_ref[rows, :]
            sin = sin_ref[rows, :]

            def rot(t):
                t1 = t[:, :ROPE_HALF]
                t2 = t[:, ROPE_HALF:]
                return jnp.concatenate([t1 * cos - t2 * sin, t1 * sin + t2 * cos], axis=-1)

            q = rot(proj_refs[4 * h][rows, :])
            k = rot(proj_refs[4 * h + 1][rows, :]) * (RET_QK_DIM ** -0.5)
            v = proj_refs[4 * h + 2][rows, :].astype(BF16)
            gate = proj_refs[4 * h + 3][rows, :]
            out.append((q.astype(BF16), (q * q_decay).astype(BF16), k.astype(BF16), (k * k_decay).astype(BF16), v,
                        gate * jax.nn.sigmoid(gate)))
        return out

    def core(h, chunks, fresh):
        chunk_decay = math.exp(c * RET_LOG_GAMMA[h])
        decay_in = decay_ref[h]
        gated = []
        for r, (qb, qd, kb, kd, v, sg) in enumerate(chunks):
            qk = _mm_nt(qb, kb) * decay_in
            state = state_ref[h]
            if r == 0:
                state = jnp.where(fresh, 0.0, state)
            cross = _mm(qd, state.astype(BF16))
            kv = _mm_tn(kd, v)
            inner = _mm(qk.astype(BF16), v)
            state_ref[h] = state * chunk_decay + kv
            y = inner + cross
            mu = jnp.mean(y, axis=-1, keepdims=True)
            yc = y - mu
            var = jnp.mean(yc * yc, axis=-1, keepdims=True)
            gated.append((sg * (yc * lax.rsqrt(var + GN_EPS))).astype(BF16))
        return jnp.concatenate(gated, axis=0)

    @pl.when(step > 0)
    def _():
        fresh = lax.rem(step - 1, tiles_per_seq) == 0
        x = xc_ref[...]
        xb_next = xn_ref[...].astype(BF16)
        loaded = load_head(0)
        project_head(0, xb_next, (0, 1))
        mix = None
        for h in range(RET_HEADS):
            gated = core(h, loaded, fresh)
            if h + 1 < RET_HEADS:
                loaded = load_head(h + 1)
                project_head(h + 1, xb_next)
            else:
                project_head(0, xb_next, (2,))
            part = _mm(gated, wo_ref[h * RET_V_DIM:(h + 1) * RET_V_DIM, :])
            mix = part if mix is None else mix + part
        project_head(0, xb_next, (3,))
        o_ref[...] = _layer_norm(DEEPNORM_ALPHA * x + mix, g_ref[...], b_ref[...])


def _ret_layer(x, w, wo, layer, cos, sin, g, b, batch, seq):
    t, d = x.shape
    tile = TOKEN_TILE
    n_tiles = t // tile
    tiles_per_seq = seq // tile
    proj_shapes = [pltpu.VMEM((tile, width), F32) for h in range(RET_HEADS) for _, width in _ret_proj_columns(h)]
    block_bytes = (3 * 2 * tile * d * 4 + (w[0].size + wo[0].size) * 2 + 2 * 2 * tile * ROPE_HALF * 4
                   + tile * RET_PROJ_DIM * 4)
    temp_bytes = (RET_HEADS * RET_QK_DIM * RET_V_DIM * 4 + RET_HEADS * RET_STEP * RET_STEP * 4
                  + tile * RET_PROJ_DIM * 4)
    prev_tile = lambda s: jnp.maximum(s - 1, 0)
    return pl.pallas_call(
        functools.partial(_ret_layer_body, tiles_per_seq),
        grid=(n_tiles + 1,),
        in_specs=[
            pl.BlockSpec((tile, d), lambda s: (prev_tile(s), 0)),
            pl.BlockSpec((tile, d), lambda s: (jnp.minimum(s, n_tiles - 1), 0)),
            _layer_block(w.shape, layer),
            _layer_block(wo.shape, layer),
            pl.BlockSpec((tile, ROPE_HALF), lambda s: (lax.rem(prev_tile(s), tiles_per_seq), 0)),
            pl.BlockSpec((tile, ROPE_HALF), lambda s: (lax.rem(prev_tile(s), tiles_per_seq), 0)),
            _resident((1, d)),
            _resident((1, d)),
        ],
        out_specs=pl.BlockSpec((tile, d), lambda s: (prev_tile(s), 0)),
        out_shape=jax.ShapeDtypeStruct((t, d), F32),
        scratch_shapes=[pltpu.VMEM((RET_HEADS, RET_QK_DIM, RET_V_DIM), F32),
                        pltpu.VMEM((RET_HEADS, RET_STEP, RET_STEP), F32)] + proj_shapes,
        compiler_params=pltpu.CompilerParams(
            dimension_semantics=("arbitrary",),
            vmem_limit_bytes=_vmem_limit(block_bytes, temp_bytes)),
        name="retention_layer",
    )(x, x, w, wo, cos, sin, g.reshape(1, d), b.reshape(1, d))


def _rope_tables(seq):
    pos = jnp.arange(seq, dtype=F32)
    freqs = 1.0 / (ROPE_BASE ** jnp.linspace(0.0, 1.0, ROPE_HALF, dtype=F32))
    ang = pos[:, None] * freqs[None, :]
    return jnp.cos(ang), jnp.sin(ang)


def kernel(x, ln_g, ln_b, ffn1_w_gate_up, ffn1_w_down, ffn2_w_gate_up, ffn2_w_down,
           attn_w_qkv, attn_b_qkv, attn_sinks, attn_w_o, attn_b_o, ret_w_qkvg, ret_w_o):
    batch, seq, d = x.shape
    assert d == D_MODEL and seq % TOKEN_TILE == 0 and TOKEN_TILE % ATTN_BLOCK == 0 and TOKEN_TILE % RET_STEP == 0
    cos, sin = _rope_tables(seq)
    w_qkv, b_qkv = _attn_qkv_layout(attn_w_qkv, attn_b_qkv)
    w_qkv = w_qkv.astype(BF16)
    attn_wo = attn_w_o.astype(BF16)
    ret_w = ret_w_qkvg.astype(BF16)
    ret_wo = ret_w_o.astype(BF16)
    h = x.reshape(batch * seq, d)
    for i in range(DEPTH):
        h = _ffn_ln(h, ffn1_w_gate_up, ffn1_w_down, i, ln_g[i, 0], ln_b[i, 0])
        j = i // N_MIXERS
        if i % N_MIXERS == 0:
            h = _attn_layer(h, w_qkv, b_qkv[j], attn_sinks[j], attn_wo, attn_b_o[j], j,
                            ln_g[i, 1], ln_b[i, 1], batch, seq)
        else:
            h = _ret_layer(h, ret_w, ret_wo, j, cos, sin, ln_g[i, 1], ln_b[i, 1], batch, seq)
        h = _ffn_ln(h, ffn2_w_gate_up, ffn2_w_down, i, ln_g[i, 2], ln_b[i, 2])
    return h.reshape(batch, seq, d)
```

```python
import functools
import math

import jax
import jax.numpy as jnp
from jax import lax
from jax.experimental import pallas as pl
from jax.experimental.pallas import tpu as pltpu

D_MODEL = 1024
DEPTH = 4
N_MIXERS = 2
ATTN_Q_HEADS = 16
ATTN_KV_HEADS = 2
ATTN_HEAD_DIM = 64
WINDOW = 128
ATTN_BLOCK = 128
RET_HEADS = 4
RET_QK_DIM = D_MODEL // RET_HEADS
RET_V_DIM = 2 * D_MODEL // RET_HEADS
ROPE_BASE = 10000.0
FFN_DIM = 2816
DEEPNORM_ALPHA = (2.0 * DEPTH) ** 0.25
LN_EPS = 1e-5
GN_EPS = 1e-6
NEG_INF = -1e30

ATTN_GROUP = ATTN_Q_HEADS // ATTN_KV_HEADS
ATTN_Q_DIM = ATTN_Q_HEADS * ATTN_HEAD_DIM
ATTN_KV_DIM = ATTN_KV_HEADS * ATTN_HEAD_DIM
RET_QK_ALL = RET_HEADS * RET_QK_DIM
RET_V_ALL = RET_HEADS * RET_V_DIM
RET_PROJ_DIM = 2 * RET_QK_ALL + 2 * RET_V_ALL
ROPE_HALF = RET_QK_DIM // 2
RET_LOG_GAMMA = tuple(math.log(1.0 - 2.0 ** (-5.0 - h)) for h in range(RET_HEADS))

V7X_LANES = 128
V7X_MXU_DIM = 256
V7X_VMEM_BYTES = 64 * 1024 * 1024
V7X_VMEM_SCOPED_MAX = 56 * 1024 * 1024

TOKEN_TILE = 512
FFN_CHUNK = V7X_MXU_DIM
RET_STEP = V7X_MXU_DIM

F32 = jnp.float32
BF16 = jnp.bfloat16


def _vmem_limit(block_bytes, temp_bytes):
    return int(min(block_bytes + temp_bytes, V7X_VMEM_SCOPED_MAX))


def _resident(shape):
    return pl.BlockSpec(shape, lambda *_: (0,) * len(shape), pipeline_mode=pl.Buffered(1))


def _layer_block(shape, layer):
    return pl.BlockSpec((None,) + tuple(shape[1:]), lambda *_: (layer,) + (0,) * (len(shape) - 1),
                        pipeline_mode=pl.Buffered(1))


def _layer_norm(y, g, b):
    mu = jnp.mean(y, axis=-1, keepdims=True)
    yc = y - mu
    var = jnp.mean(yc * yc, axis=-1, keepdims=True)
    return yc * lax.rsqrt(var + LN_EPS) * g + b


def _mm(a, b):
    return jnp.dot(a, b, preferred_element_type=F32)


def _mm_nt(a, b):
    return lax.dot_general(a, b, (((1,), (1,)), ((), ())), preferred_element_type=F32)


def _mm_tn(a, b):
    return lax.dot_general(a, b, (((0,), (0,)), ((), ())), preferred_element_type=F32)


FFN_N_CHUNKS = FFN_DIM // FFN_CHUNK


def _ffn_weight_copies(layer, c, wgu_hbm, wd_hbm, wgu_ref, wd_ref, sem):
    lo = c * FFN_CHUNK
    return (
        pltpu.make_async_copy(wgu_hbm.at[layer, :, pl.ds(lo, FFN_CHUNK)],
                              wgu_ref.at[:, pl.ds(lo, FFN_CHUNK)], sem.at[0, c]),
        pltpu.make_async_copy(wgu_hbm.at[layer, :, pl.ds(FFN_DIM + lo, FFN_CHUNK)],
                              wgu_ref.at[:, pl.ds(FFN_DIM + lo, FFN_CHUNK)], sem.at[1, c]),
        pltpu.make_async_copy(wd_hbm.at[layer, pl.ds(lo, FFN_CHUNK), :],
                              wd_ref.at[pl.ds(lo, FFN_CHUNK), :], sem.at[2, c]),
    )


def _ffn_ln_body(layer, x_ref, wgu_hbm, wd_hbm, g_ref, b_ref, o_ref, wgu_ref, wd_ref, sem):
    step = pl.program_id(0)

    def copies(c):
        return _ffn_weight_copies(layer, c, wgu_hbm, wd_hbm, wgu_ref, wd_ref, sem)

    def tile(wait_for_weights):
        x = x_ref[...]
        xb = x.astype(BF16)
        acc = None
        for c in range(FFN_N_CHUNKS):
            lo = c * FFN_CHUNK
            if wait_for_weights:
                for cp in copies(c):
                    cp.wait()
            gate = _mm(xb, wgu_ref[:, lo:lo + FFN_CHUNK].astype(BF16))
            up = _mm(xb, wgu_ref[:, FFN_DIM + lo:FFN_DIM + lo + FFN_CHUNK].astype(BF16))
            h = (gate * jax.nn.sigmoid(gate) * up).astype(BF16)
            part = _mm(h, wd_ref[lo:lo + FFN_CHUNK, :].astype(BF16))
            acc = part if acc is None else acc + part
        y = DEEPNORM_ALPHA * x + 0.5 * acc
        o_ref[...] = _layer_norm(y, g_ref[...], b_ref[...])

    @pl.when(step == 0)
    def _():
        for c in range(FFN_N_CHUNKS):
            for cp in copies(c):
                cp.start()
        tile(wait_for_weights=True)

    @pl.when(step > 0)
    def _():
        tile(wait_for_weights=False)


def _ffn_ln(x, w_gate_up, w_down, layer, g, b):
    t, d = x.shape
    tm = TOKEN_TILE
    block_bytes = 2 * 2 * tm * d * 4 + (w_gate_up[0].size + w_down[0].size) * 4
    temp_bytes = 6 * tm * d * 4
    return pl.pallas_call(
        functools.partial(_ffn_ln_body, layer),
        grid=(t // tm,),
        in_specs=[
            pl.BlockSpec((tm, d), lambda i: (i, 0)),
            pl.BlockSpec(memory_space=pl.ANY),
            pl.BlockSpec(memory_space=pl.ANY),
            _resident((1, d)),
            _resident((1, d)),
        ],
        out_specs=pl.BlockSpec((tm, d), lambda i: (i, 0)),
        out_shape=jax.ShapeDtypeStruct((t, d), F32),
        scratch_shapes=[pltpu.VMEM(w_gate_up.shape[1:], F32), pltpu.VMEM(w_down.shape[1:], F32),
                        pltpu.SemaphoreType.DMA((3, FFN_N_CHUNKS))],
        compiler_params=pltpu.CompilerParams(
            dimension_semantics=("arbitrary",),
            vmem_limit_bytes=_vmem_limit(block_bytes, temp_bytes)),
        name="ffn_ln",
    )(x, w_gate_up, w_down, g.reshape(1, d), b.reshape(1, d))


ATTN_PAIR_LANES = 2 * ATTN_HEAD_DIM
ATTN_PAIRS_PER_KV = ATTN_GROUP // 2
ATTN_KV_DUP = ATTN_KV_HEADS * ATTN_PAIR_LANES
ATTN_PROJ_DIM = ATTN_Q_DIM + 2 * ATTN_KV_DUP
assert WINDOW == ATTN_BLOCK and ATTN_PAIR_LANES == V7X_LANES and ATTN_GROUP % 2 == 0


def _swa_scores(q, kv_cur, kv_prev, j):
    blk = ATTN_BLOCK
    low = lax.broadcasted_iota(jnp.int32, (2 * blk, ATTN_PAIR_LANES), 1) < ATTN_HEAD_DIM
    zero = jnp.zeros((), BF16)
    ko = j * ATTN_PAIR_LANES
    vo = ATTN_KV_DUP + j * ATTN_PAIR_LANES
    kk = jnp.concatenate([kv_prev[:, ko:ko + ATTN_PAIR_LANES], kv_cur[:, ko:ko + ATTN_PAIR_LANES]], axis=0)
    vv = jnp.concatenate([kv_prev[:, vo:vo + ATTN_PAIR_LANES], kv_cur[:, vo:vo + ATTN_PAIR_LANES]], axis=0)
    k_bd = jnp.concatenate([jnp.where(low, kk, zero), jnp.where(low, zero, kk)], axis=0)
    v_bd = jnp.concatenate([jnp.where(low, vv, zero), jnp.where(low, zero, vv)], axis=0)
    qo = j * ATTN_GROUP * ATTN_HEAD_DIM
    q_st = jnp.concatenate(
        [q[:, qo + p * ATTN_PAIR_LANES:qo + (p + 1) * ATTN_PAIR_LANES] for p in range(ATTN_PAIRS_PER_KV)],
        axis=0)
    return _mm_nt(q_st, k_bd), v_bd


def _swa_finish(s, v_bd, sink_ref, first, j):
    blk = ATTN_BLOCK
    qi = lax.broadcasted_iota(jnp.int32, (blk, blk), 0)
    kc = lax.broadcasted_iota(jnp.int32, (blk, blk), 1)
    from_prev = kc > qi
    lane_q = lax.broadcasted_iota(jnp.int32, (blk, ATTN_PAIR_LANES), 1) < ATTN_HEAD_DIM
    zero = jnp.zeros((), BF16)
    p_rows = []
    inv = []
    for p in range(ATTN_PAIRS_PER_KV):
        tiles = []
        for par in range(2):
            sink = sink_ref[j * ATTN_GROUP + 2 * p + par]
            base = par * 2 * blk
            s_prev = s[p * blk:(p + 1) * blk, base:base + blk]
            s_cur = s[p * blk:(p + 1) * blk, base + blk:base + 2 * blk]
            if first is not None:
                s_prev = jnp.where(first, NEG_INF, s_prev)
            logit = jnp.where(from_prev, s_prev, s_cur)
            m = jnp.maximum(jnp.max(logit, axis=-1, keepdims=True), sink)
            e = jnp.exp(logit - m)
            denom = jnp.sum(e, axis=-1, keepdims=True) + jnp.exp(sink - m)
            inv.append(1.0 / denom)
            eb = e.astype(BF16)
            tiles += [jnp.where(from_prev, eb, zero), jnp.where(from_prev, zero, eb)]
        p_rows.append(jnp.concatenate(tiles, axis=1))
    o = _mm(jnp.concatenate(p_rows, axis=0), v_bd)
    out_tiles = []
    for p in range(ATTN_PAIRS_PER_KV):
        scale = jnp.where(lane_q, inv[2 * p], inv[2 * p + 1])
        out_tiles.append((o[p * blk:(p + 1) * blk, :] * scale).astype(BF16))
    return out_tiles


def _attn_layer_body(sink_ref, x_ref, w_ref, bqkv_ref, wo_ref, bo_ref, g_ref, b_ref, o_ref, kv_ref):
    blk = ATTN_BLOCK
    tile = x_ref.shape[0]
    at_start = pl.program_id(1) == 0

    @pl.when(at_start)
    def _():
        kv_ref[...] = jnp.zeros_like(kv_ref)

    x = x_ref[...]
    xb = x.astype(BF16)
    q = ((_mm(xb, w_ref[:, :ATTN_Q_DIM]) + bqkv_ref[:, :ATTN_Q_DIM]) * (ATTN_HEAD_DIM ** -0.5)).astype(BF16)
    kv = (_mm(xb, w_ref[:, ATTN_Q_DIM:]) + bqkv_ref[:, ATTN_Q_DIM:]).astype(BF16)
    units = [(r, j) for r in range(tile // blk) for j in range(ATTN_KV_HEADS)]

    def scores(u):
        r, j = units[u]
        rows = slice(r * blk, (r + 1) * blk)
        kv_prev = kv_ref[...].astype(BF16) if r == 0 else kv[(r - 1) * blk:r * blk]
        return _swa_scores(q[rows], kv[rows], kv_prev, j)

    out_tiles = {}
    pending = scores(0)
    for u, (r, j) in enumerate(units):
        s, v_bd = pending
        if u + 1 < len(units):
            pending = scores(u + 1)
        out_tiles[(r, j)] = _swa_finish(s, v_bd, sink_ref, at_start if r == 0 else None, j)
    kv_ref[...] = kv[tile - blk:].astype(F32)
    heads = jnp.concatenate(
        [jnp.concatenate([t for j in range(ATTN_KV_HEADS) for t in out_tiles[(r, j)]], axis=1)
         for r in range(tile // blk)], axis=0)
    mix = _mm(heads, wo_ref[...]) + bo_ref[...]
    o_ref[...] = _layer_norm(DEEPNORM_ALPHA * x + mix, g_ref[...], b_ref[...])


def _attn_layer(x, w_qkv, b_qkv, sinks, wo, bo, layer, g, b, batch, seq):
    d = x.shape[-1]
    tile = TOKEN_TILE
    x = x.reshape(batch, seq, d)
    block_bytes = 2 * 2 * tile * d * 4 + (w_qkv[0].size + wo[0].size) * 2
    temp_bytes = 12 * tile * d * 4
    out = pl.pallas_call(
        _attn_layer_body,
        grid=(batch, seq // tile),
        in_specs=[
            pl.BlockSpec(memory_space=pltpu.SMEM),
            pl.BlockSpec((None, tile, d), lambda bi, n: (bi, n, 0)),
            _layer_block(w_qkv.shape, layer),
            _resident((1, ATTN_PROJ_DIM)),
            _layer_block(wo.shape, layer),
            _resident((1, d)),
            _resident((1, d)),
            _resident((1, d)),
        ],
        out_specs=pl.BlockSpec((None, tile, d), lambda bi, n: (bi, n, 0)),
        out_shape=jax.ShapeDtypeStruct((batch, seq, d), F32),
        scratch_shapes=[pltpu.VMEM((ATTN_BLOCK, 2 * ATTN_KV_DUP), F32)],
        compiler_params=pltpu.CompilerParams(
            dimension_semantics=("parallel", "arbitrary"),
            vmem_limit_bytes=_vmem_limit(block_bytes, temp_bytes)),
        name="attention_layer",
    )(sinks, x, w_qkv, b_qkv.reshape(1, ATTN_PROJ_DIM), wo, bo.reshape(1, d), g.reshape(1, d), b.reshape(1, d))
    return out.reshape(batch * seq, d)


def _attn_qkv_layout(w_qkv, b_qkv):
    def dup(t, off):
        parts = []
        for j in range(ATTN_KV_HEADS):
            col = t[..., off + j * ATTN_HEAD_DIM:off + (j + 1) * ATTN_HEAD_DIM]
            parts += [col, col]
        return parts
    def relayout(t):
        return jnp.concatenate([t[..., :ATTN_Q_DIM]] + dup(t, ATTN_Q_DIM) + dup(t, ATTN_Q_DIM + ATTN_KV_DIM), axis=-1)
    return relayout(w_qkv), relayout(b_qkv)


def _ret_proj_columns(h):
    return ((h * RET_QK_DIM, RET_QK_DIM),
            (RET_QK_ALL + h * RET_QK_DIM, RET_QK_DIM),
            (2 * RET_QK_ALL + h * RET_V_DIM, RET_V_DIM),
            (2 * RET_QK_ALL + RET_V_ALL + h * RET_V_DIM, RET_V_DIM))


def _ret_layer_body(tiles_per_seq, xc_ref, xn_ref, w_ref, wo_ref, cos_ref, sin_ref, g_ref, b_ref, o_ref,
                    state_ref, decay_ref, *proj_refs):
    c = RET_STEP
    n_chunks = xc_ref.shape[0] // c
    step = pl.program_id(0)

    def project_head(h, xb, parts=(0, 1, 2, 3)):
        columns = _ret_proj_columns(h)
        for i in parts:
            lo, width = columns[i]
            proj_refs[4 * h + i][...] = _mm(xb, w_ref[:, lo:lo + width])

    @pl.when(step == 0)
    def _():
        state_ref[...] = jnp.zeros_like(state_ref)
        row = lax.broadcasted_iota(jnp.int32, (c, c), 0).astype(F32)
        col = lax.broadcasted_iota(jnp.int32, (c, c), 1).astype(F32)
        diff = row - col
        for h in range(RET_HEADS):
            decay_ref[h] = jnp.where(diff >= 0, jnp.exp(jnp.maximum(diff, 0.0) * RET_LOG_GAMMA[h]), 0.0)
        xb = xn_ref[...].astype(BF16)
        for h in range(RET_HEADS):
            project_head(h, xb)

    def load_head(h):
        lg = RET_LOG_GAMMA[h]
        idx = lax.broadcasted_iota(jnp.int32, (c, 1), 0).astype(F32)
        q_decay = jnp.exp((idx + 1.0) * lg)
        k_decay = jnp.exp((c - 1.0 - idx) * lg)
        out = []
        for r in range(n_chunks):
            rows = slice(r * c, (r + 1) * c)
            cos = cos_ref[rows, :]
            sin = sin_ref[rows, :]

            def rot(t):
                t1 = t[:, :ROPE_HALF]
                t2 = t[:, ROPE_HALF:]
                return jnp.concatenate([t1 * cos - t2 * sin, t1 * sin + t2 * cos], axis=-1)

            q = rot(proj_refs[4 * h][rows, :])
            k = rot(proj_refs[4 * h + 1][rows, :]) * (RET_QK_DIM ** -0.5)
            v = proj_refs[4 * h + 2][rows, :].astype(BF16)
            gate = proj_refs[4 * h + 3][rows, :]
            out.append((q.astype(BF16), (q * q_decay).astype(BF16), k.astype(BF16), (k * k_decay).astype(BF16), v,
                        gate * jax.nn.sigmoid(gate)))
        return out

    def core(h, chunks, fresh):
        chunk_decay = math.exp(c * RET_LOG_GAMMA[h])
        decay_in = decay_ref[h]
        gated = []
        for r, (qb, qd, kb, kd, v, sg) in enumerate(chunks):
            qk = _mm_nt(qb, kb) * decay_in
            state = state_ref[h]
            if r == 0:
                state = jnp.where(fresh, 0.0, state)
            cross = _mm(qd, state.astype(BF16))
            kv = _mm_tn(kd, v)
            inner = _mm(qk.astype(BF16), v)
            state_ref[h] = state * chunk_decay + kv
            y = inner + cross
            mu = jnp.mean(y, axis=-1, keepdims=True)
            yc = y - mu
            var = jnp.mean(yc * yc, axis=-1, keepdims=True)
            gated.append((sg * (yc * lax.rsqrt(var + GN_EPS))).astype(BF16))
        return jnp.concatenate(gated, axis=0)

    @pl.when(step > 0)
    def _():
        fresh = lax.rem(step - 1, tiles_per_seq) == 0
        x = xc_ref[...]
        xb_next = xn_ref[...].astype(BF16)
        loaded = load_head(0)
        project_head(0, xb_next, (0, 1))
        mix = None
        for h in range(RET_HEADS):
            gated = core(h, loaded, fresh)
            if h + 1 < RET_HEADS:
                loaded = load_head(h + 1)
                project_head(h + 1, xb_next)
            else:
                project_head(0, xb_next, (2,))
            part = _mm(gated, wo_ref[h * RET_V_DIM:(h + 1) * RET_V_DIM, :])
            mix = part if mix is None else mix + part
        project_head(0, xb_next, (3,))
        o_ref[...] = _layer_norm(DEEPNORM_ALPHA * x + mix, g_ref[...], b_ref[...])


def _ret_layer(x, w, wo, layer, cos, sin, g, b, seq):
    t, d = x.shape
    tile = TOKEN_TILE
    n_tiles = t // tile
    tiles_per_seq = seq // tile
    proj_shapes = [pltpu.VMEM((tile, width), F32) for h in range(RET_HEADS) for _, width in _ret_proj_columns(h)]
    block_bytes = (3 * 2 * tile * d * 4 + (w[0].size + wo[0].size) * 2 + 2 * 2 * tile * ROPE_HALF * 4
                   + tile * RET_PROJ_DIM * 4)
    temp_bytes = (RET_HEADS * RET_QK_DIM * RET_V_DIM * 4 + RET_HEADS * RET_STEP * RET_STEP * 4
                  + tile * RET_PROJ_DIM * 4)
    prev_tile = lambda s: jnp.maximum(s - 1, 0)
    return pl.pallas_call(
        functools.partial(_ret_layer_body, tiles_per_seq),
        grid=(n_tiles + 1,),
        in_specs=[
            pl.BlockSpec((tile, d), lambda s: (prev_tile(s), 0)),
            pl.BlockSpec((tile, d), lambda s: (jnp.minimum(s, n_tiles - 1), 0)),
            _layer_block(w.shape, layer),
            _layer_block(wo.shape, layer),
            pl.BlockSpec((tile, ROPE_HALF), lambda s: (lax.rem(prev_tile(s), tiles_per_seq), 0)),
            pl.BlockSpec((tile, ROPE_HALF), lambda s: (lax.rem(prev_tile(s), tiles_per_seq), 0)),
            _resident((1, d)),
            _resident((1, d)),
        ],
        out_specs=pl.BlockSpec((tile, d), lambda s: (prev_tile(s), 0)),
        out_shape=jax.ShapeDtypeStruct((t, d), F32),
        scratch_shapes=[pltpu.VMEM((RET_HEADS, RET_QK_DIM, RET_V_DIM), F32),
                        pltpu.VMEM((RET_HEADS, RET_STEP, RET_STEP), F32)] + proj_shapes,
        compiler_params=pltpu.CompilerParams(
            dimension_semantics=("arbitrary",),
            vmem_limit_bytes=_vmem_limit(block_bytes, temp_bytes)),
        name="retention_layer",
    )(x, x, w, wo, cos, sin, g.reshape(1, d), b.reshape(1, d))


def _rope_tables(seq):
    pos = jnp.arange(seq, dtype=F32)
    freqs = 1.0 / (ROPE_BASE ** jnp.linspace(0.0, 1.0, ROPE_HALF, dtype=F32))
    ang = pos[:, None] * freqs[None, :]
    return jnp.cos(ang), jnp.sin(ang)


def kernel(x, ln_g, ln_b, ffn1_w_gate_up, ffn1_w_down, ffn2_w_gate_up, ffn2_w_down,
           attn_w_qkv, attn_b_qkv, attn_sinks, attn_w_o, attn_b_o, ret_w_qkvg, ret_w_o):
    batch, seq, d = x.shape
    assert d == D_MODEL and seq % TOKEN_TILE == 0 and TOKEN_TILE % ATTN_BLOCK == 0 and TOKEN_TILE % RET_STEP == 0
    cos, sin = _rope_tables(seq)
    w_qkv, b_qkv = _attn_qkv_layout(attn_w_qkv, attn_b_qkv)
    w_qkv = w_qkv.astype(BF16)
    attn_wo = attn_w_o.astype(BF16)
    ret_w = ret_w_qkvg.astype(BF16)
    ret_wo = ret_w_o.astype(BF16)
    h = x.reshape(batch * seq, d)
    for i in range(DEPTH):
        h = _ffn_ln(h, ffn1_w_gate_up, ffn1_w_down, i, ln_g[i, 0], ln_b[i, 0])
        j = i // N_MIXERS
        if i % N_MIXERS == 0:
            h = _attn_layer(h, w_qkv, b_qkv[j], attn_sinks[j], attn_wo, attn_b_o[j], j,
                            ln_g[i, 1], ln_b[i, 1], batch, seq)
        else:
            h = _ret_layer(h, ret_w, ret_wo, j, cos, sin, ln_g[i, 1], ln_b[i, 1], seq)
        h = _ffn_ln(h, ffn2_w_gate_up, ffn2_w_down, i, ln_g[i, 2], ln_b[i, 2])
    return h.reshape(batch, seq, d)
```

```python
import functools
import math

import jax
import jax.numpy as jnp
from jax import lax
from jax.experimental import pallas as pl
from jax.experimental.pallas import tpu as pltpu

D_MODEL = 1024
DEPTH = 4
N_MIXERS = 2
ATTN_Q_HEADS = 16
ATTN_KV_HEADS = 2
ATTN_HEAD_DIM = 64
WINDOW = 128
ATTN_BLOCK = 128
RET_HEADS = 4
RET_QK_DIM = D_MODEL // RET_HEADS
RET_V_DIM = 2 * D_MODEL // RET_HEADS
ROPE_BASE = 10000.0
FFN_DIM = 2816
DEEPNORM_ALPHA = (2.0 * DEPTH) ** 0.25
LN_EPS = 1e-5
GN_EPS = 1e-6
NEG_INF = -1e30

ATTN_GROUP = ATTN_Q_HEADS // ATTN_KV_HEADS
ATTN_Q_DIM = ATTN_Q_HEADS * ATTN_HEAD_DIM
ATTN_KV_DIM = ATTN_KV_HEADS * ATTN_HEAD_DIM
RET_QK_ALL = RET_HEADS * RET_QK_DIM
RET_V_ALL = RET_HEADS * RET_V_DIM
RET_PROJ_DIM = 2 * RET_QK_ALL + 2 * RET_V_ALL
ROPE_HALF = RET_QK_DIM // 2
RET_LOG_GAMMA = tuple(math.log(1.0 - 2.0 ** (-5.0 - h)) for h in range(RET_HEADS))

V7X_LANES = 128
V7X_MXU_DIM = 256
V7X_VMEM_BYTES = 64 * 1024 * 1024
V7X_VMEM_SCOPED_MAX = 56 * 1024 * 1024

TOKEN_TILE = 512
FFN_CHUNK = V7X_MXU_DIM
RET_STEP = V7X_MXU_DIM

F32 = jnp.float32
BF16 = jnp.bfloat16


def _vmem_limit(block_bytes, temp_bytes):
    return int(min(block_bytes + temp_bytes, V7X_VMEM_SCOPED_MAX))


def _resident(shape):
    return pl.BlockSpec(shape, lambda *_: (0,) * len(shape), pipeline_mode=pl.Buffered(1))


def _layer_block(shape, layer):
    return pl.BlockSpec((None,) + tuple(shape[1:]), lambda *_: (layer,) + (0,) * (len(shape) - 1),
                        pipeline_mode=pl.Buffered(1))


def _layer_norm(y, g, b):
    mu = jnp.mean(y, axis=-1, keepdims=True)
    yc = y - mu
    var = jnp.mean(yc * yc, axis=-1, keepdims=True)
    return yc * lax.rsqrt(var + LN_EPS) * g + b


def _mm(a, b):
    return jnp.dot(a, b, preferred_element_type=F32)


def _mm_nt(a, b):
    return lax.dot_general(a, b, (((1,), (1,)), ((), ())), preferred_element_type=F32)


def _mm_tn(a, b):
    return lax.dot_general(a, b, (((0,), (0,)), ((), ())), preferred_element_type=F32)


FFN_N_CHUNKS = FFN_DIM // FFN_CHUNK


def _ffn_weight_copies(layer, c, wgu_hbm, wd_hbm, wgu_ref, wd_ref, sem):
    lo = c * FFN_CHUNK
    return (
        pltpu.make_async_copy(wgu_hbm.at[layer, :, pl.ds(lo, FFN_CHUNK)],
                              wgu_ref.at[:, pl.ds(lo, FFN_CHUNK)], sem.at[0, c]),
        pltpu.make_async_copy(wgu_hbm.at[layer, :, pl.ds(FFN_DIM + lo, FFN_CHUNK)],
                              wgu_ref.at[:, pl.ds(FFN_DIM + lo, FFN_CHUNK)], sem.at[1, c]),
        pltpu.make_async_copy(wd_hbm.at[layer, pl.ds(lo, FFN_CHUNK), :],
                              wd_ref.at[pl.ds(lo, FFN_CHUNK), :], sem.at[2, c]),
    )


def _ffn_ln_body(layer, x_ref, wgu_hbm, wd_hbm, g_ref, b_ref, o_ref, wgu_ref, wd_ref, sem):
    step = pl.program_id(0)

    def copies(c):
        return _ffn_weight_copies(layer, c, wgu_hbm, wd_hbm, wgu_ref, wd_ref, sem)

    def tile(wait_for_weights):
        x = x_ref[...]
        xb = x.astype(BF16)
        acc = None
        for c in range(FFN_N_CHUNKS):
            lo = c * FFN_CHUNK
            if wait_for_weights:
                for cp in copies(c):
                    cp.wait()
            gate = _mm(xb, wgu_ref[:, lo:lo + FFN_CHUNK].astype(BF16))
            up = _mm(xb, wgu_ref[:, FFN_DIM + lo:FFN_DIM + lo + FFN_CHUNK].astype(BF16))
            h = (gate * jax.nn.sigmoid(gate) * up).astype(BF16)
            part = _mm(h, wd_ref[lo:lo + FFN_CHUNK, :].astype(BF16))
            acc = part if acc is None else acc + part
        y = DEEPNORM_ALPHA * x + 0.5 * acc
        o_ref[...] = _layer_norm(y, g_ref[...], b_ref[...])

    @pl.when(step == 0)
    def _():
        for c in range(FFN_N_CHUNKS):
            for cp in copies(c):
                cp.start()
        tile(wait_for_weights=True)

    @pl.when(step > 0)
    def _():
        tile(wait_for_weights=False)


def _ffn_ln(x, w_gate_up, w_down, layer, g, b):
    t, d = x.shape
    tm = TOKEN_TILE
    block_bytes = 2 * 2 * tm * d * 4 + (w_gate_up[0].size + w_down[0].size) * 4
    temp_bytes = 6 * tm * d * 4
    return pl.pallas_call(
        functools.partial(_ffn_ln_body, layer),
        grid=(t // tm,),
        in_specs=[
            pl.BlockSpec((tm, d), lambda i: (i, 0)),
            pl.BlockSpec(memory_space=pl.ANY),
            pl.BlockSpec(memory_space=pl.ANY),
            _resident((1, d)),
            _resident((1, d)),
        ],
        out_specs=pl.BlockSpec((tm, d), lambda i: (i, 0)),
        out_shape=jax.ShapeDtypeStruct((t, d), F32),
        scratch_shapes=[pltpu.VMEM(w_gate_up.shape[1:], F32), pltpu.VMEM(w_down.shape[1:], F32),
                        pltpu.SemaphoreType.DMA((3, FFN_N_CHUNKS))],
        compiler_params=pltpu.CompilerParams(
            dimension_semantics=("arbitrary",),
            vmem_limit_bytes=_vmem_limit(block_bytes, temp_bytes)),
        name="ffn_ln",
    )(x, w_gate_up, w_down, g.reshape(1, d), b.reshape(1, d))


ATTN_PAIR_LANES = 2 * ATTN_HEAD_DIM
ATTN_PAIRS_PER_KV = ATTN_GROUP // 2
ATTN_KV_DUP = ATTN_KV_HEADS * ATTN_PAIR_LANES
ATTN_PROJ_DIM = ATTN_Q_DIM + 2 * ATTN_KV_DUP
assert WINDOW == ATTN_BLOCK and ATTN_PAIR_LANES == V7X_LANES and ATTN_GROUP % 2 == 0


def _swa_scores(q, kv_cur, kv_prev, j):
    blk = ATTN_BLOCK
    low = lax.broadcasted_iota(jnp.int32, (2 * blk, ATTN_PAIR_LANES), 1) < ATTN_HEAD_DIM
    zero = jnp.zeros((), BF16)
    ko = j * ATTN_PAIR_LANES
    vo = ATTN_KV_DUP + j * ATTN_PAIR_LANES
    kk = jnp.concatenate([kv_prev[:, ko:ko + ATTN_PAIR_LANES], kv_cur[:, ko:ko + ATTN_PAIR_LANES]], axis=0)
    vv = jnp.concatenate([kv_prev[:, vo:vo + ATTN_PAIR_LANES], kv_cur[:, vo:vo + ATTN_PAIR_LANES]], axis=0)
    k_bd = jnp.concatenate([jnp.where(low, kk, zero), jnp.where(low, zero, kk)], axis=0)
    v_bd = jnp.concatenate([jnp.where(low, vv, zero), jnp.where(low, zero, vv)], axis=0)
    qo = j * ATTN_GROUP * ATTN_HEAD_DIM
    q_st = jnp.concatenate(
        [q[:, qo + p * ATTN_PAIR_LANES:qo + (p + 1) * ATTN_PAIR_LANES] for p in range(ATTN_PAIRS_PER_KV)],
        axis=0)
    return _mm_nt(q_st, k_bd), v_bd


def _swa_finish(s, v_bd, sink_ref, first, j):
    blk = ATTN_BLOCK
    qi = lax.broadcasted_iota(jnp.int32, (blk, blk), 0)
    kc = lax.broadcasted_iota(jnp.int32, (blk, blk), 1)
    from_prev = kc > qi
    lane_q = lax.broadcasted_iota(jnp.int32, (blk, ATTN_PAIR_LANES), 1) < ATTN_HEAD_DIM
    zero = jnp.zeros((), BF16)
    p_rows = []
    inv = []
    for p in range(ATTN_PAIRS_PER_KV):
        tiles = []
        for par in range(2):
            sink = sink_ref[j * ATTN_GROUP + 2 * p + par]
            base = par * 2 * blk
            s_prev = s[p * blk:(p + 1) * blk, base:base + blk]
            s_cur = s[p * blk:(p + 1) * blk, base + blk:base + 2 * blk]
            if first is not None:
                s_prev = jnp.where(first, NEG_INF, s_prev)
            logit = jnp.where(from_prev, s_prev, s_cur)
            m = jnp.maximum(jnp.max(logit, axis=-1, keepdims=True), sink)
            e = jnp.exp(logit - m)
            denom = jnp.sum(e, axis=-1, keepdims=True) + jnp.exp(sink - m)
            inv.append(1.0 / denom)
            eb = e.astype(BF16)
            tiles += [jnp.where(from_prev, eb, zero), jnp.where(from_prev, zero, eb)]
        p_rows.append(jnp.concatenate(tiles, axis=1))
    o = _mm(jnp.concatenate(p_rows, axis=0), v_bd)
    out_tiles = []
    for p in range(ATTN_PAIRS_PER_KV):
        scale = jnp.where(lane_q, inv[2 * p], inv[2 * p + 1])
        out_tiles.append((o[p * blk:(p + 1) * blk, :] * scale).astype(BF16))
    return out_tiles


def _attn_layer_body(sink_ref, x_ref, w_ref, bqkv_ref, wo_ref, bo_ref, g_ref, b_ref, o_ref, kv_ref):
    blk = ATTN_BLOCK
    tile = x_ref.shape[0]
    at_start = pl.program_id(1) == 0

    @pl.when(at_start)
    def _():
        kv_ref[...] = jnp.zeros_like(kv_ref)

    x = x_ref[...]
    xb = x.astype(BF16)
    q = ((_mm(xb, w_ref[:, :ATTN_Q_DIM]) + bqkv_ref[:, :ATTN_Q_DIM]) * (ATTN_HEAD_DIM ** -0.5)).astype(BF16)
    kv = (_mm(xb, w_ref[:, ATTN_Q_DIM:]) + bqkv_ref[:, ATTN_Q_DIM:]).astype(BF16)
    units = [(r, j) for r in range(tile // blk) for j in range(ATTN_KV_HEADS)]

    def scores(u):
        r, j = units[u]
        rows = slice(r * blk, (r + 1) * blk)
        kv_prev = kv_ref[...].astype(BF16) if r == 0 else kv[(r - 1) * blk:r * blk]
        return _swa_scores(q[rows], kv[rows], kv_prev, j)

    out_tiles = {}
    pending = scores(0)
    for u, (r, j) in enumerate(units):
        s, v_bd = pending
        if u + 1 < len(units):
            pending = scores(u + 1)
        out_tiles[(r, j)] = _swa_finish(s, v_bd, sink_ref, at_start if r == 0 else None, j)
    kv_ref[...] = kv[tile - blk:].astype(F32)
    heads = jnp.concatenate(
        [jnp.concatenate([t for j in range(ATTN_KV_HEADS) for t in out_tiles[(r, j)]], axis=1)
         for r in range(tile // blk)], axis=0)
    mix = _mm(heads, wo_ref[...]) + bo_ref[...]
    o_ref[...] = _layer_norm(DEEPNORM_ALPHA * x + mix, g_ref[...], b_ref[...])


def _attn_layer(x, w_qkv, b_qkv, sinks, wo, bo, layer, g, b, batch, seq):
    d = x.shape[-1]
    tile = TOKEN_TILE
    x = x.reshape(batch, seq, d)
    block_bytes = 2 * 2 * tile * d * 4 + (w_qkv[0].size + wo[0].size) * 2
    temp_bytes = 12 * tile * d * 4
    out = pl.pallas_call(
        _attn_layer_body,
        grid=(batch, seq // tile),
        in_specs=[
            pl.BlockSpec(memory_space=pltpu.SMEM),
            pl.BlockSpec((None, tile, d), lambda bi, n: (bi, n, 0)),
            _layer_block(w_qkv.shape, layer),
            _resident((1, ATTN_PROJ_DIM)),
            _layer_block(wo.shape, layer),
            _resident((1, d)),
            _resident((1, d)),
            _resident((1, d)),
        ],
        out_specs=pl.BlockSpec((None, tile, d), lambda bi, n: (bi, n, 0)),
        out_shape=jax.ShapeDtypeStruct((batch, seq, d), F32),
        scratch_shapes=[pltpu.VMEM((ATTN_BLOCK, 2 * ATTN_KV_DUP), F32)],
        compiler_params=pltpu.CompilerParams(
            dimension_semantics=("parallel", "arbitrary"),
            vmem_limit_bytes=_vmem_limit(block_bytes, temp_bytes)),
        name="attention_layer",
    )(sinks, x, w_qkv, b_qkv.reshape(1, ATTN_PROJ_DIM), wo, bo.reshape(1, d), g.reshape(1, d), b.reshape(1, d))
    return out.reshape(batch * seq, d)


def _attn_qkv_layout(w_qkv, b_qkv):
    def dup(t, off):
        parts = []
        for j in range(ATTN_KV_HEADS):
            col = t[..., off + j * ATTN_HEAD_DIM:off + (j + 1) * ATTN_HEAD_DIM]
            parts += [col, col]
        return parts
    def relayout(t):
        return jnp.concatenate([t[..., :ATTN_Q_DIM]] + dup(t, ATTN_Q_DIM) + dup(t, ATTN_Q_DIM + ATTN_KV_DIM), axis=-1)
    return relayout(w_qkv), relayout(b_qkv)


def _ret_proj_columns(h):
    return ((h * RET_QK_DIM, RET_QK_DIM),
            (RET_QK_ALL + h * RET_QK_DIM, RET_QK_DIM),
            (2 * RET_QK_ALL + h * RET_V_DIM, RET_V_DIM),
            (2 * RET_QK_ALL + RET_V_ALL + h * RET_V_DIM, RET_V_DIM))


def _ret_layer_body(tiles_per_seq, xc_ref, xn_ref, w_ref, wo_ref, cos_ref, sin_ref, g_ref, b_ref, o_ref,
                    state_ref, decay_ref, *proj_refs):
    c = RET_STEP
    n_chunks = xc_ref.shape[0] // c
    step = pl.program_id(0)

    def project(h, xb, parts=(0, 1, 2, 3)):
        columns = _ret_proj_columns(h)
        return [_mm(xb, w_ref[:, columns[i][0]:columns[i][0] + columns[i][1]].astype(BF16)) for i in parts]

    def park_head0(xb, parts):
        for i, value in zip(parts, project(0, xb, parts)):
            proj_refs[i][...] = value

    @pl.when(step == 0)
    def _():
        state_ref[...] = jnp.zeros_like(state_ref)
        row = lax.broadcasted_iota(jnp.int32, (c, c), 0).astype(F32)
        col = lax.broadcasted_iota(jnp.int32, (c, c), 1).astype(F32)
        diff = row - col
        for h in range(RET_HEADS):
            decay_ref[h] = jnp.where(diff >= 0, jnp.exp(jnp.maximum(diff, 0.0) * RET_LOG_GAMMA[h]), 0.0)
        park_head0(xn_ref[...].astype(BF16), (0, 1, 2, 3))

    def load_head(h, proj):
        lg = RET_LOG_GAMMA[h]
        idx = lax.broadcasted_iota(jnp.int32, (c, 1), 0).astype(F32)
        q_decay = jnp.exp((idx + 1.0) * lg)
        k_decay = jnp.exp((c - 1.0 - idx) * lg)
        out = []
        for r in range(n_chunks):
            rows = slice(r * c, (r + 1) * c)
            cos = cos_ref[rows, :]
            sin = sin_ref[rows, :]

            def rot(t):
                t1 = t[:, :ROPE_HALF]
                t2 = t[:, ROPE_HALF:]
                return jnp.concatenate([t1 * cos - t2 * sin, t1 * sin + t2 * cos], axis=-1)

            q = rot(proj[0][rows, :])
            k = rot(proj[1][rows, :]) * (RET_QK_DIM ** -0.5)
            v = proj[2][rows, :].astype(BF16)
            gate = proj[3][rows, :]
            out.append((q.astype(BF16), (q * q_decay).astype(BF16), k.astype(BF16), (k * k_decay).astype(BF16), v,
                        gate * jax.nn.sigmoid(gate)))
        return out

    def core(h, chunks, fresh):
        chunk_decay = math.exp(c * RET_LOG_GAMMA[h])
        decay_in = decay_ref[h]
        gated = []
        for r, (qb, qd, kb, kd, v, sg) in enumerate(chunks):
            qk = _mm_nt(qb, kb) * decay_in
            state = state_ref[h]
            if r == 0:
                state = jnp.where(fresh, 0.0, state)
            cross = _mm(qd, state.astype(BF16))
            kv = _mm_tn(kd, v)
            inner = _mm(qk.astype(BF16), v)
            state_ref[h] = state * chunk_decay + kv
            y = inner + cross
            mu = jnp.mean(y, axis=-1, keepdims=True)
            yc = y - mu
            var = jnp.mean(yc * yc, axis=-1, keepdims=True)
            gated.append((sg * (yc * lax.rsqrt(var + GN_EPS))).astype(BF16))
        return jnp.concatenate(gated, axis=0)

    @pl.when(step > 0)
    def _():
        fresh = lax.rem(step - 1, tiles_per_seq) == 0
        x = xc_ref[...]
        xb = x.astype(BF16)
        xb_next = xn_ref[...].astype(BF16)
        loaded = load_head(0, proj_refs)
        upcoming = project(1, xb)
        mix = None
        for h in range(RET_HEADS):
            gated = core(h, loaded, fresh)
            if h + 1 < RET_HEADS:
                loaded = load_head(h + 1, upcoming)
            if h + 2 < RET_HEADS:
                upcoming = project(h + 2, xb)
            elif h + 2 == RET_HEADS:
                park_head0(xb_next, (0, 1))
            else:
                park_head0(xb_next, (2,))
            part = _mm(gated, wo_ref[h * RET_V_DIM:(h + 1) * RET_V_DIM, :])
            mix = part if mix is None else mix + part
        park_head0(xb_next, (3,))
        o_ref[...] = _layer_norm(DEEPNORM_ALPHA * x + mix, g_ref[...], b_ref[...])


def _ret_layer(x, w, wo, layer, cos, sin, g, b, seq):
    t, d = x.shape
    tile = TOKEN_TILE
    n_tiles = t // tile
    tiles_per_seq = seq // tile
    proj_shapes = [pltpu.VMEM((tile, width), F32) for _, width in _ret_proj_columns(0)]
    head_bytes = tile * (RET_PROJ_DIM // RET_HEADS) * 4
    block_bytes = (3 * 2 * tile * d * 4 + w[0].size * 4 + wo[0].size * 2 + 2 * 2 * tile * ROPE_HALF * 4
                   + head_bytes)
    temp_bytes = (RET_HEADS * RET_QK_DIM * RET_V_DIM * 4 + RET_HEADS * RET_STEP * RET_STEP * 4
                  + 3 * head_bytes)
    prev_tile = lambda s: jnp.maximum(s - 1, 0)
    return pl.pallas_call(
        functools.partial(_ret_layer_body, tiles_per_seq),
        grid=(n_tiles + 1,),
        in_specs=[
            pl.BlockSpec((tile, d), lambda s: (prev_tile(s), 0)),
            pl.BlockSpec((tile, d), lambda s: (jnp.minimum(s, n_tiles - 1), 0)),
            _layer_block(w.shape, layer),
            _layer_block(wo.shape, layer),
            pl.BlockSpec((tile, ROPE_HALF), lambda s: (lax.rem(prev_tile(s), tiles_per_seq), 0)),
            pl.BlockSpec((tile, ROPE_HALF), lambda s: (lax.rem(prev_tile(s), tiles_per_seq), 0)),
            _resident((1, d)),
            _resident((1, d)),
        ],
        out_specs=pl.BlockSpec((tile, d), lambda s: (prev_tile(s), 0)),
        out_shape=jax.ShapeDtypeStruct((t, d), F32),
        scratch_shapes=[pltpu.VMEM((RET_HEADS, RET_QK_DIM, RET_V_DIM), F32),
                        pltpu.VMEM((RET_HEADS, RET_STEP, RET_STEP), F32)] + proj_shapes,
        compiler_params=pltpu.CompilerParams(
            dimension_semantics=("arbitrary",),
            vmem_limit_bytes=_vmem_limit(block_bytes, temp_bytes)),
        name="retention_layer",
    )(x, x, w, wo, cos, sin, g.reshape(1, d), b.reshape(1, d))


def _rope_tables(seq):
    pos = jnp.arange(seq, dtype=F32)
    freqs = 1.0 / (ROPE_BASE ** jnp.linspace(0.0, 1.0, ROPE_HALF, dtype=F32))
    ang = pos[:, None] * freqs[None, :]
    return jnp.cos(ang), jnp.sin(ang)


def kernel(x, ln_g, ln_b, ffn1_w_gate_up, ffn1_w_down, ffn2_w_gate_up, ffn2_w_down,
           attn_w_qkv, attn_b_qkv, attn_sinks, attn_w_o, attn_b_o, ret_w_qkvg, ret_w_o):
    batch, seq, d = x.shape
    assert d == D_MODEL and seq % TOKEN_TILE == 0 and TOKEN_TILE % ATTN_BLOCK == 0 and TOKEN_TILE % RET_STEP == 0
    cos, sin = _rope_tables(seq)
    w_qkv, b_qkv = _attn_qkv_layout(attn_w_qkv, attn_b_qkv)
    w_qkv = w_qkv.astype(BF16)
    attn_wo = attn_w_o.astype(BF16)
    ret_wo = ret_w_o.astype(BF16)
    h = x.reshape(batch * seq, d)
    for i in range(DEPTH):
        h = _ffn_ln(h, ffn1_w_gate_up, ffn1_w_down, i, ln_g[i, 0], ln_b[i, 0])
        j = i // N_MIXERS
        if i % N_MIXERS == 0:
            h = _attn_layer(h, w_qkv, b_qkv[j], attn_sinks[j], attn_wo, attn_b_o[j], j,
                            ln_g[i, 1], ln_b[i, 1], batch, seq)
        else:
            h = _ret_layer(h, ret_w_qkvg, ret_wo, j, cos, sin, ln_g[i, 1], ln_b[i, 1], seq)
        h = _ffn_ln(h, ffn2_w_gate_up, ffn2_w_down, i, ln_g[i, 2], ln_b[i, 2])
    return h.reshape(batch, seq, d)
```

```python
import functools
import math

import jax
import jax.numpy as jnp
from jax import lax
from jax.experimental import pallas as pl
from jax.experimental.pallas import tpu as pltpu

D_MODEL = 1024
DEPTH = 4
N_MIXERS = 2
ATTN_Q_HEADS = 16
ATTN_KV_HEADS = 2
ATTN_HEAD_DIM = 64
WINDOW = 128
ATTN_BLOCK = 128
RET_HEADS = 4
RET_QK_DIM = D_MODEL // RET_HEADS
RET_V_DIM = 2 * D_MODEL // RET_HEADS
ROPE_BASE = 10000.0
FFN_DIM = 2816
DEEPNORM_ALPHA = (2.0 * DEPTH) ** 0.25
LN_EPS = 1e-5
GN_EPS = 1e-6
NEG_INF = -1e30

ATTN_GROUP = ATTN_Q_HEADS // ATTN_KV_HEADS
ATTN_Q_DIM = ATTN_Q_HEADS * ATTN_HEAD_DIM
ATTN_KV_DIM = ATTN_KV_HEADS * ATTN_HEAD_DIM
RET_QK_ALL = RET_HEADS * RET_QK_DIM
RET_V_ALL = RET_HEADS * RET_V_DIM
RET_PROJ_DIM = 2 * RET_QK_ALL + 2 * RET_V_ALL
ROPE_HALF = RET_QK_DIM // 2
RET_LOG_GAMMA = tuple(math.log(1.0 - 2.0 ** (-5.0 - h)) for h in range(RET_HEADS))

V7X_LANES = 128
V7X_MXU_DIM = 256
V7X_VMEM_BYTES = 64 * 1024 * 1024
V7X_VMEM_SCOPED_MAX = 56 * 1024 * 1024

TOKEN_TILE = 512
FFN_CHUNK = V7X_MXU_DIM
RET_STEP = V7X_MXU_DIM

F32 = jnp.float32
BF16 = jnp.bfloat16


def _vmem_limit(block_bytes, temp_bytes):
    return int(min(block_bytes + temp_bytes, V7X_VMEM_SCOPED_MAX))


def _resident(shape):
    return pl.BlockSpec(shape, lambda *_: (0,) * len(shape), pipeline_mode=pl.Buffered(1))


def _layer_block(shape, layer):
    return pl.BlockSpec((None,) + tuple(shape[1:]), lambda *_: (layer,) + (0,) * (len(shape) - 1),
                        pipeline_mode=pl.Buffered(1))


def _layer_norm(y, g, b):
    mu = jnp.mean(y, axis=-1, keepdims=True)
    yc = y - mu
    var = jnp.mean(yc * yc, axis=-1, keepdims=True)
    return yc * lax.rsqrt(var + LN_EPS) * g + b


def _mm(a, b):
    return jnp.dot(a, b, preferred_element_type=F32)


def _mm_nt(a, b):
    return lax.dot_general(a, b, (((1,), (1,)), ((), ())), preferred_element_type=F32)


def _mm_tn(a, b):
    return lax.dot_general(a, b, (((0,), (0,)), ((), ())), preferred_element_type=F32)


FFN_N_CHUNKS = FFN_DIM // FFN_CHUNK


def _ffn_weight_copies(layer, c, wgu_hbm, wd_hbm, wgu_ref, wd_ref, sem):
    lo = c * FFN_CHUNK
    return (
        pltpu.make_async_copy(wgu_hbm.at[layer, :, pl.ds(lo, FFN_CHUNK)],
                              wgu_ref.at[:, pl.ds(lo, FFN_CHUNK)], sem.at[0, c]),
        pltpu.make_async_copy(wgu_hbm.at[layer, :, pl.ds(FFN_DIM + lo, FFN_CHUNK)],
                              wgu_ref.at[:, pl.ds(FFN_DIM + lo, FFN_CHUNK)], sem.at[1, c]),
        pltpu.make_async_copy(wd_hbm.at[layer, pl.ds(lo, FFN_CHUNK), :],
                              wd_ref.at[pl.ds(lo, FFN_CHUNK), :], sem.at[2, c]),
    )


def _ffn_ln_body(layer, x_ref, wgu_hbm, wd_hbm, g_ref, b_ref, o_ref, wgu_ref, wd_ref, sem):
    step = pl.program_id(0)

    def copies(c):
        return _ffn_weight_copies(layer, c, wgu_hbm, wd_hbm, wgu_ref, wd_ref, sem)

    def tile(wait_for_weights):
        x = x_ref[...]
        xb = x.astype(BF16)
        acc = None
        for c in range(FFN_N_CHUNKS):
            lo = c * FFN_CHUNK
            if wait_for_weights:
                for cp in copies(c):
                    cp.wait()
            gate = _mm(xb, wgu_ref[:, lo:lo + FFN_CHUNK].astype(BF16))
            up = _mm(xb, wgu_ref[:, FFN_DIM + lo:FFN_DIM + lo + FFN_CHUNK].astype(BF16))
            h = (gate * jax.nn.sigmoid(gate) * up).astype(BF16)
            part = _mm(h, wd_ref[lo:lo + FFN_CHUNK, :].astype(BF16))
            acc = part if acc is None else acc + part
        y = DEEPNORM_ALPHA * x + 0.5 * acc
        o_ref[...] = _layer_norm(y, g_ref[...], b_ref[...])

    @pl.when(step == 0)
    def _():
        for c in range(FFN_N_CHUNKS):
            for cp in copies(c):
                cp.start()
        tile(wait_for_weights=True)

    @pl.when(step > 0)
    def _():
        tile(wait_for_weights=False)


def _ffn_ln(x, w_gate_up, w_down, layer, g, b):
    t, d = x.shape
    tm = TOKEN_TILE
    block_bytes = 2 * 2 * tm * d * 4 + (w_gate_up[0].size + w_down[0].size) * 4
    temp_bytes = 6 * tm * d * 4
    return pl.pallas_call(
        functools.partial(_ffn_ln_body, layer),
        grid=(t // tm,),
        in_specs=[
            pl.BlockSpec((tm, d), lambda i: (i, 0)),
            pl.BlockSpec(memory_space=pl.ANY),
            pl.BlockSpec(memory_space=pl.ANY),
            _resident((1, d)),
            _resident((1, d)),
        ],
        out_specs=pl.BlockSpec((tm, d), lambda i: (i, 0)),
        out_shape=jax.ShapeDtypeStruct((t, d), F32),
        scratch_shapes=[pltpu.VMEM(w_gate_up.shape[1:], F32), pltpu.VMEM(w_down.shape[1:], F32),
                        pltpu.SemaphoreType.DMA((3, FFN_N_CHUNKS))],
        compiler_params=pltpu.CompilerParams(
            dimension_semantics=("arbitrary",),
            vmem_limit_bytes=_vmem_limit(block_bytes, temp_bytes)),
        name="ffn_ln",
    )(x, w_gate_up, w_down, g.reshape(1, d), b.reshape(1, d))


ATTN_PAIR_LANES = 2 * ATTN_HEAD_DIM
ATTN_PAIRS_PER_KV = ATTN_GROUP // 2
ATTN_PROJ_DIM = ATTN_Q_DIM + 2 * ATTN_KV_DIM
assert WINDOW == ATTN_BLOCK and ATTN_PAIR_LANES == V7X_LANES and ATTN_GROUP % 2 == 0 and ATTN_KV_HEADS == 2


def _swa_scores(q, kv_cur, kv_prev, j):
    blk = ATTN_BLOCK
    low = lax.broadcasted_iota(jnp.int32, (2 * blk, ATTN_PAIR_LANES), 1) < ATTN_HEAD_DIM
    zero = jnp.zeros((), BF16)

    def block_diagonal(t):
        swapped = pltpu.roll(t, ATTN_HEAD_DIM, 1)
        in_low, in_high = (t, swapped) if j == 0 else (swapped, t)
        return jnp.concatenate([jnp.where(low, in_low, zero), jnp.where(low, zero, in_high)], axis=0)

    k_bd = block_diagonal(jnp.concatenate([kv_prev[:, :ATTN_KV_DIM], kv_cur[:, :ATTN_KV_DIM]], axis=0))
    v_bd = block_diagonal(jnp.concatenate([kv_prev[:, ATTN_KV_DIM:], kv_cur[:, ATTN_KV_DIM:]], axis=0))
    qo = j * ATTN_GROUP * ATTN_HEAD_DIM
    q_st = jnp.concatenate(
        [q[:, qo + p * ATTN_PAIR_LANES:qo + (p + 1) * ATTN_PAIR_LANES] for p in range(ATTN_PAIRS_PER_KV)],
        axis=0)
    return _mm_nt(q_st, k_bd), v_bd


def _swa_finish(s, v_bd, sink_ref, first, j):
    blk = ATTN_BLOCK
    qi = lax.broadcasted_iota(jnp.int32, (blk, blk), 0)
    kc = lax.broadcasted_iota(jnp.int32, (blk, blk), 1)
    from_prev = kc > qi
    lane_q = lax.broadcasted_iota(jnp.int32, (blk, ATTN_PAIR_LANES), 1) < ATTN_HEAD_DIM
    zero = jnp.zeros((), BF16)
    p_rows = []
    inv = []
    for p in range(ATTN_PAIRS_PER_KV):
        tiles = []
        for par in range(2):
            sink = sink_ref[j * ATTN_GROUP + 2 * p + par]
            base = par * 2 * blk
            s_prev = s[p * blk:(p + 1) * blk, base:base + blk]
            s_cur = s[p * blk:(p + 1) * blk, base + blk:base + 2 * blk]
            if first is not None:
                s_prev = jnp.where(first, NEG_INF, s_prev)
            logit = jnp.where(from_prev, s_prev, s_cur)
            m = jnp.maximum(jnp.max(logit, axis=-1, keepdims=True), sink)
            e = jnp.exp(logit - m)
            denom = jnp.sum(e, axis=-1, keepdims=True) + jnp.exp(sink - m)
            inv.append(1.0 / denom)
            eb = e.astype(BF16)
            tiles += [jnp.where(from_prev, eb, zero), jnp.where(from_prev, zero, eb)]
        p_rows.append(jnp.concatenate(tiles, axis=1))
    o = _mm(jnp.concatenate(p_rows, axis=0), v_bd)
    out_tiles = []
    for p in range(ATTN_PAIRS_PER_KV):
        scale = jnp.where(lane_q, inv[2 * p], inv[2 * p + 1])
        out_tiles.append((o[p * blk:(p + 1) * blk, :] * scale).astype(BF16))
    return out_tiles


def _attn_layer_body(sink_ref, x_ref, w_ref, bqkv_ref, wo_ref, bo_ref, g_ref, b_ref, o_ref, kv_ref):
    blk = ATTN_BLOCK
    tile = x_ref.shape[0]
    at_start = pl.program_id(1) == 0

    @pl.when(at_start)
    def _():
        kv_ref[...] = jnp.zeros_like(kv_ref)

    x = x_ref[...]
    xb = x.astype(BF16)
    kv = (_mm(xb, w_ref[:, ATTN_Q_DIM:].astype(BF16)) + bqkv_ref[:, ATTN_Q_DIM:]).astype(BF16)
    q = ((_mm(xb, w_ref[:, :ATTN_Q_DIM].astype(BF16)) + bqkv_ref[:, :ATTN_Q_DIM])
         * (ATTN_HEAD_DIM ** -0.5)).astype(BF16)
    units = [(r, j) for r in range(tile // blk) for j in range(ATTN_KV_HEADS)]

    def scores(u):
        r, j = units[u]
        rows = slice(r * blk, (r + 1) * blk)
        kv_prev = kv_ref[...].astype(BF16) if r == 0 else kv[(r - 1) * blk:r * blk]
        return _swa_scores(q[rows], kv[rows], kv_prev, j)

    out_tiles = {}
    pending = scores(0)
    for u, (r, j) in enumerate(units):
        s, v_bd = pending
        if u + 1 < len(units):
            pending = scores(u + 1)
        out_tiles[(r, j)] = _swa_finish(s, v_bd, sink_ref, at_start if r == 0 else None, j)
    kv_ref[...] = kv[tile - blk:].astype(F32)
    heads = jnp.concatenate(
        [jnp.concatenate([t for j in range(ATTN_KV_HEADS) for t in out_tiles[(r, j)]], axis=1)
         for r in range(tile // blk)], axis=0)
    mix = _mm(heads, wo_ref[...]) + bo_ref[...]
    o_ref[...] = _layer_norm(DEEPNORM_ALPHA * x + mix, g_ref[...], b_ref[...])


def _attn_layer(x, w_qkv, b_qkv, sinks, wo, bo, layer, g, b, batch, seq):
    d = x.shape[-1]
    tile = TOKEN_TILE
    x = x.reshape(batch, seq, d)
    block_bytes = 2 * 2 * tile * d * 4 + w_qkv[0].size * 4 + wo[0].size * 2
    temp_bytes = 12 * tile * d * 4
    out = pl.pallas_call(
        _attn_layer_body,
        grid=(batch, seq // tile),
        in_specs=[
            pl.BlockSpec(memory_space=pltpu.SMEM),
            pl.BlockSpec((None, tile, d), lambda bi, n: (bi, n, 0)),
            _layer_block(w_qkv.shape, layer),
            _resident((1, ATTN_PROJ_DIM)),
            _layer_block(wo.shape, layer),
            _resident((1, d)),
            _resident((1, d)),
            _resident((1, d)),
        ],
        out_specs=pl.BlockSpec((None, tile, d), lambda bi, n: (bi, n, 0)),
        out_shape=jax.ShapeDtypeStruct((batch, seq, d), F32),
        scratch_shapes=[pltpu.VMEM((ATTN_BLOCK, 2 * ATTN_KV_DIM), F32)],
        compiler_params=pltpu.CompilerParams(
            dimension_semantics=("parallel", "arbitrary"),
            vmem_limit_bytes=_vmem_limit(block_bytes, temp_bytes)),
        name="attention_layer",
    )(sinks, x, w_qkv, b_qkv.reshape(1, ATTN_PROJ_DIM), wo, bo.reshape(1, d), g.reshape(1, d), b.reshape(1, d))
    return out.reshape(batch * seq, d)


def _ret_proj_columns(h):
    return ((h * RET_QK_DIM, RET_QK_DIM),
            (RET_QK_ALL + h * RET_QK_DIM, RET_QK_DIM),
            (2 * RET_QK_ALL + h * RET_V_DIM, RET_V_DIM),
            (2 * RET_QK_ALL + RET_V_ALL + h * RET_V_DIM, RET_V_DIM))


def _ret_layer_body(tiles_per_seq, xc_ref, xn_ref, w_ref, wo_ref, cos_ref, sin_ref, g_ref, b_ref, o_ref,
                    state_ref, decay_ref, *proj_refs):
    c = RET_STEP
    n_chunks = xc_ref.shape[0] // c
    step = pl.program_id(0)

    def project(h, xb, parts=(0, 1, 2, 3)):
        columns = _ret_proj_columns(h)
        return [_mm(xb, w_ref[:, columns[i][0]:columns[i][0] + columns[i][1]].astype(BF16)) for i in parts]

    def park_head0(xb, parts):
        for i, value in zip(parts, project(0, xb, parts)):
            proj_refs[i][...] = value

    @pl.when(step == 0)
    def _():
        state_ref[...] = jnp.zeros_like(state_ref)
        row = lax.broadcasted_iota(jnp.int32, (c, c), 0).astype(F32)
        col = lax.broadcasted_iota(jnp.int32, (c, c), 1).astype(F32)
        diff = row - col
        for h in range(RET_HEADS):
            decay_ref[h] = jnp.where(diff >= 0, jnp.exp(jnp.maximum(diff, 0.0) * RET_LOG_GAMMA[h]), 0.0)
        park_head0(xn_ref[...].astype(BF16), (0, 1, 2, 3))

    def load_head(h, proj):
        lg = RET_LOG_GAMMA[h]
        idx = lax.broadcasted_iota(jnp.int32, (c, 1), 0).astype(F32)
        q_decay = jnp.exp((idx + 1.0) * lg)
        k_decay = jnp.exp((c - 1.0 - idx) * lg)
        out = []
        for r in range(n_chunks):
            rows = slice(r * c, (r + 1) * c)
            cos = cos_ref[rows, :]
            sin = sin_ref[rows, :]

            def rot(t):
                t1 = t[:, :ROPE_HALF]
                t2 = t[:, ROPE_HALF:]
                return jnp.concatenate([t1 * cos - t2 * sin, t1 * sin + t2 * cos], axis=-1)

            q = rot(proj[0][rows, :])
            k = rot(proj[1][rows, :]) * (RET_QK_DIM ** -0.5)
            v = proj[2][rows, :].astype(BF16)
            gate = proj[3][rows, :]
            out.append((q.astype(BF16), (q * q_decay).astype(BF16), k.astype(BF16), (k * k_decay).astype(BF16), v,
                        gate * jax.nn.sigmoid(gate)))
        return out

    def core(h, chunks, fresh):
        chunk_decay = math.exp(c * RET_LOG_GAMMA[h])
        decay_in = decay_ref[h]
        gated = []
        for r, (qb, qd, kb, kd, v, sg) in enumerate(chunks):
            qk = _mm_nt(qb, kb) * decay_in
            state = state_ref[h]
            if r == 0:
                state = jnp.where(fresh, 0.0, state)
            cross = _mm(qd, state.astype(BF16))
            kv = _mm_tn(kd, v)
            inner = _mm(qk.astype(BF16), v)
            state_ref[h] = state * chunk_decay + kv
            y = inner + cross
            mu = jnp.mean(y, axis=-1, keepdims=True)
            yc = y - mu
            var = jnp.mean(yc * yc, axis=-1, keepdims=True)
            gated.append((sg * (yc * lax.rsqrt(var + GN_EPS))).astype(BF16))
        return jnp.concatenate(gated, axis=0)

    @pl.when(step > 0)
    def _():
        fresh = lax.rem(step - 1, tiles_per_seq) == 0
        x = xc_ref[...]
        xb = x.astype(BF16)
        xb_next = xn_ref[...].astype(BF16)
        loaded = load_head(0, proj_refs)
        upcoming = project(1, xb)
        mix = None
        for h in range(RET_HEADS):
            gated = core(h, loaded, fresh)
            if h + 1 < RET_HEADS:
                loaded = load_head(h + 1, upcoming)
            if h + 2 < RET_HEADS:
                upcoming = project(h + 2, xb)
            elif h + 2 == RET_HEADS:
                park_head0(xb_next, (0, 1))
            else:
                park_head0(xb_next, (2,))
            part = _mm(gated, wo_ref[h * RET_V_DIM:(h + 1) * RET_V_DIM, :])
            mix = part if mix is None else mix + part
        park_head0(xb_next, (3,))
        o_ref[...] = _layer_norm(DEEPNORM_ALPHA * x + mix, g_ref[...], b_ref[...])


def _ret_layer(x, w, wo, layer, cos, sin, g, b, seq):
    t, d = x.shape
    tile = TOKEN_TILE
    n_tiles = t // tile
    tiles_per_seq = seq // tile
    proj_shapes = [pltpu.VMEM((tile, width), F32) for _, width in _ret_proj_columns(0)]
    head_bytes = tile * (RET_PROJ_DIM // RET_HEADS) * 4
    block_bytes = (3 * 2 * tile * d * 4 + w[0].size * 4 + wo[0].size * 2 + 2 * 2 * tile * ROPE_HALF * 4
                   + head_bytes)
    temp_bytes = (RET_HEADS * RET_QK_DIM * RET_V_DIM * 4 + RET_HEADS * RET_STEP * RET_STEP * 4
                  + 3 * head_bytes)
    prev_tile = lambda s: jnp.maximum(s - 1, 0)
    return pl.pallas_call(
        functools.partial(_ret_layer_body, tiles_per_seq),
        grid=(n_tiles + 1,),
        in_specs=[
            pl.BlockSpec((tile, d), lambda s: (prev_tile(s), 0)),
            pl.BlockSpec((tile, d), lambda s: (jnp.minimum(s, n_tiles - 1), 0)),
            _layer_block(w.shape, layer),
            _layer_block(wo.shape, layer),
            pl.BlockSpec((tile, ROPE_HALF), lambda s: (lax.rem(prev_tile(s), tiles_per_seq), 0)),
            pl.BlockSpec((tile, ROPE_HALF), lambda s: (lax.rem(prev_tile(s), tiles_per_seq), 0)),
            _resident((1, d)),
            _resident((1, d)),
        ],
        out_specs=pl.BlockSpec((tile, d), lambda s: (prev_tile(s), 0)),
        out_shape=jax.ShapeDtypeStruct((t, d), F32),
        scratch_shapes=[pltpu.VMEM((RET_HEADS, RET_QK_DIM, RET_V_DIM), F32),
                        pltpu.VMEM((RET_HEADS, RET_STEP, RET_STEP), F32)] + proj_shapes,
        compiler_params=pltpu.CompilerParams(
            dimension_semantics=("arbitrary",),
            vmem_limit_bytes=_vmem_limit(block_bytes, temp_bytes)),
        name="retention_layer",
    )(x, x, w, wo, cos, sin, g.reshape(1, d), b.reshape(1, d))


def _rope_tables(seq):
    pos = jnp.arange(seq, dtype=F32)
    freqs = 1.0 / (ROPE_BASE ** jnp.linspace(0.0, 1.0, ROPE_HALF, dtype=F32))
    ang = pos[:, None] * freqs[None, :]
    return jnp.cos(ang), jnp.sin(ang)


def kernel(x, ln_g, ln_b, ffn1_w_gate_up, ffn1_w_down, ffn2_w_gate_up, ffn2_w_down,
           attn_w_qkv, attn_b_qkv, attn_sinks, attn_w_o, attn_b_o, ret_w_qkvg, ret_w_o):
    batch, seq, d = x.shape
    assert d == D_MODEL and seq % TOKEN_TILE == 0 and TOKEN_TILE % ATTN_BLOCK == 0 and TOKEN_TILE % RET_STEP == 0
    cos, sin = _rope_tables(seq)
    attn_wo = attn_w_o.astype(BF16)
    ret_wo = ret_w_o.astype(BF16)
    h = x.reshape(batch * seq, d)
    for i in range(DEPTH):
        h = _ffn_ln(h, ffn1_w_gate_up, ffn1_w_down, i, ln_g[i, 0], ln_b[i, 0])
        j = i // N_MIXERS
        if i % N_MIXERS == 0:
            h = _attn_layer(h, attn_w_qkv, attn_b_qkv[j], attn_sinks[j], attn_wo, attn_b_o[j], j,
                            ln_g[i, 1], ln_b[i, 1], batch, seq)
        else:
            h = _ret_layer(h, ret_w_qkvg, ret_wo, j, cos, sin, ln_g[i, 1], ln_b[i, 1], seq)
        h = _ffn_ln(h, ffn2_w_gate_up, ffn2_w_down, i, ln_g[i, 2], ln_b[i, 2])
    return h.reshape(batch, seq, d)
```

```python
import functools
import math

import jax
import jax.numpy as jnp
from jax import lax
from jax.experimental import pallas as pl
from jax.experimental.pallas import tpu as pltpu

D_MODEL = 1024
DEPTH = 4
N_MIXERS = 2
ATTN_Q_HEADS = 16
ATTN_KV_HEADS = 2
ATTN_HEAD_DIM = 64
WINDOW = 128
ATTN_BLOCK = 128
RET_HEADS = 4
RET_QK_DIM = D_MODEL // RET_HEADS
RET_V_DIM = 2 * D_MODEL // RET_HEADS
ROPE_BASE = 10000.0
FFN_DIM = 2816
DEEPNORM_ALPHA = (2.0 * DEPTH) ** 0.25
LN_EPS = 1e-5
GN_EPS = 1e-6
NEG_INF = -1e30

ATTN_GROUP = ATTN_Q_HEADS // ATTN_KV_HEADS
ATTN_Q_DIM = ATTN_Q_HEADS * ATTN_HEAD_DIM
ATTN_KV_DIM = ATTN_KV_HEADS * ATTN_HEAD_DIM
RET_QK_ALL = RET_HEADS * RET_QK_DIM
RET_V_ALL = RET_HEADS * RET_V_DIM
RET_PROJ_DIM = 2 * RET_QK_ALL + 2 * RET_V_ALL
ROPE_HALF = RET_QK_DIM // 2
RET_LOG_GAMMA = tuple(math.log(1.0 - 2.0 ** (-5.0 - h)) for h in range(RET_HEADS))

V7X_LANES = 128
V7X_MXU_DIM = 256
V7X_VMEM_BYTES = 64 * 1024 * 1024
V7X_VMEM_SCOPED_MAX = 56 * 1024 * 1024

TOKEN_TILE = 512
ATTN_TILE = 1024
FFN_CHUNK = V7X_MXU_DIM
RET_STEP = V7X_MXU_DIM

F32 = jnp.float32
BF16 = jnp.bfloat16


def _vmem_limit(block_bytes, temp_bytes):
    return int(min(block_bytes + temp_bytes, V7X_VMEM_SCOPED_MAX))


def _resident(shape):
    return pl.BlockSpec(shape, lambda *_: (0,) * len(shape), pipeline_mode=pl.Buffered(1))


def _layer_block(shape, layer):
    return pl.BlockSpec((None,) + tuple(shape[1:]), lambda *_: (layer,) + (0,) * (len(shape) - 1),
                        pipeline_mode=pl.Buffered(1))


def _layer_norm(y, g, b):
    mu = jnp.mean(y, axis=-1, keepdims=True)
    yc = y - mu
    var = jnp.mean(yc * yc, axis=-1, keepdims=True)
    return yc * lax.rsqrt(var + LN_EPS) * g + b


def _mm(a, b):
    return jnp.dot(a, b, preferred_element_type=F32)


def _mm_nt(a, b):
    return lax.dot_general(a, b, (((1,), (1,)), ((), ())), preferred_element_type=F32)


def _mm_tn(a, b):
    return lax.dot_general(a, b, (((0,), (0,)), ((), ())), preferred_element_type=F32)


FFN_N_CHUNKS = FFN_DIM // FFN_CHUNK


def _ffn_weight_copies(layer, c, wgu_hbm, wd_hbm, wgu_ref, wd_ref, sem):
    lo = c * FFN_CHUNK
    return (
        pltpu.make_async_copy(wgu_hbm.at[layer, :, pl.ds(lo, FFN_CHUNK)],
                              wgu_ref.at[:, pl.ds(lo, FFN_CHUNK)], sem.at[0, c]),
        pltpu.make_async_copy(wgu_hbm.at[layer, :, pl.ds(FFN_DIM + lo, FFN_CHUNK)],
                              wgu_ref.at[:, pl.ds(FFN_DIM + lo, FFN_CHUNK)], sem.at[1, c]),
        pltpu.make_async_copy(wd_hbm.at[layer, pl.ds(lo, FFN_CHUNK), :],
                              wd_ref.at[pl.ds(lo, FFN_CHUNK), :], sem.at[2, c]),
    )


def _ffn_ln_body(layer, x_ref, wgu_hbm, wd_hbm, g_ref, b_ref, o_ref, wgu_ref, wd_ref, sem):
    step = pl.program_id(0)

    def copies(c):
        return _ffn_weight_copies(layer, c, wgu_hbm, wd_hbm, wgu_ref, wd_ref, sem)

    def tile(wait_for_weights):
        x = x_ref[...]
        xb = x.astype(BF16)
        acc = None
        for c in range(FFN_N_CHUNKS):
            lo = c * FFN_CHUNK
            if wait_for_weights:
                for cp in copies(c):
                    cp.wait()
            gate = _mm(xb, wgu_ref[:, lo:lo + FFN_CHUNK].astype(BF16))
            up = _mm(xb, wgu_ref[:, FFN_DIM + lo:FFN_DIM + lo + FFN_CHUNK].astype(BF16))
            h = (gate * jax.nn.sigmoid(gate) * up).astype(BF16)
            part = _mm(h, wd_ref[lo:lo + FFN_CHUNK, :].astype(BF16))
            acc = part if acc is None else acc + part
        y = DEEPNORM_ALPHA * x + 0.5 * acc
        o_ref[...] = _layer_norm(y, g_ref[...], b_ref[...])

    @pl.when(step == 0)
    def _():
        for c in range(FFN_N_CHUNKS):
            for cp in copies(c):
                cp.start()
        tile(wait_for_weights=True)

    @pl.when(step > 0)
    def _():
        tile(wait_for_weights=False)


def _ffn_ln(x, w_gate_up, w_down, layer, g, b):
    t, d = x.shape
    tm = TOKEN_TILE
    block_bytes = 2 * 2 * tm * d * 4 + (w_gate_up[0].size + w_down[0].size) * 4
    temp_bytes = 6 * tm * d * 4
    return pl.pallas_call(
        functools.partial(_ffn_ln_body, layer),
        grid=(t // tm,),
        in_specs=[
            pl.BlockSpec((tm, d), lambda i: (i, 0)),
            pl.BlockSpec(memory_space=pl.ANY),
            pl.BlockSpec(memory_space=pl.ANY),
            _resident((1, d)),
            _resident((1, d)),
        ],
        out_specs=pl.BlockSpec((tm, d), lambda i: (i, 0)),
        out_shape=jax.ShapeDtypeStruct((t, d), F32),
        scratch_shapes=[pltpu.VMEM(w_gate_up.shape[1:], F32), pltpu.VMEM(w_down.shape[1:], F32),
                        pltpu.SemaphoreType.DMA((3, FFN_N_CHUNKS))],
        compiler_params=pltpu.CompilerParams(
            dimension_semantics=("arbitrary",),
            vmem_limit_bytes=_vmem_limit(block_bytes, temp_bytes)),
        name="ffn_ln",
    )(x, w_gate_up, w_down, g.reshape(1, d), b.reshape(1, d))


ATTN_PAIR_LANES = 2 * ATTN_HEAD_DIM
ATTN_PAIRS_PER_KV = ATTN_GROUP // 2
ATTN_PROJ_DIM = ATTN_Q_DIM + 2 * ATTN_KV_DIM
assert WINDOW == ATTN_BLOCK and ATTN_PAIR_LANES == V7X_LANES and ATTN_GROUP % 2 == 0 and ATTN_KV_HEADS == 2


def _swa_scores(q, kv_cur, kv_prev, j):
    blk = ATTN_BLOCK
    low = lax.broadcasted_iota(jnp.int32, (2 * blk, ATTN_PAIR_LANES), 1) < ATTN_HEAD_DIM
    zero = jnp.zeros((), BF16)

    def block_diagonal(t):
        swapped = pltpu.roll(t, ATTN_HEAD_DIM, 1)
        in_low, in_high = (t, swapped) if j == 0 else (swapped, t)
        return jnp.concatenate([jnp.where(low, in_low, zero), jnp.where(low, zero, in_high)], axis=0)

    k_bd = block_diagonal(jnp.concatenate([kv_prev[:, :ATTN_KV_DIM], kv_cur[:, :ATTN_KV_DIM]], axis=0))
    v_bd = block_diagonal(jnp.concatenate([kv_prev[:, ATTN_KV_DIM:], kv_cur[:, ATTN_KV_DIM:]], axis=0))
    qo = j * ATTN_GROUP * ATTN_HEAD_DIM
    q_st = jnp.concatenate(
        [q[:, qo + p * ATTN_PAIR_LANES:qo + (p + 1) * ATTN_PAIR_LANES] for p in range(ATTN_PAIRS_PER_KV)],
        axis=0)
    return _mm_nt(q_st, k_bd), v_bd


def _swa_finish(s, v_bd, sink_ref, first, j):
    blk = ATTN_BLOCK
    qi = lax.broadcasted_iota(jnp.int32, (blk, blk), 0)
    kc = lax.broadcasted_iota(jnp.int32, (blk, blk), 1)
    from_prev = kc > qi
    lane_q = lax.broadcasted_iota(jnp.int32, (blk, ATTN_PAIR_LANES), 1) < ATTN_HEAD_DIM
    zero = jnp.zeros((), BF16)
    p_rows = []
    inv = []
    for p in range(ATTN_PAIRS_PER_KV):
        tiles = []
        for par in range(2):
            sink = sink_ref[j * ATTN_GROUP + 2 * p + par]
            base = par * 2 * blk
            s_prev = s[p * blk:(p + 1) * blk, base:base + blk]
            s_cur = s[p * blk:(p + 1) * blk, base + blk:base + 2 * blk]
            if first is not None:
                s_prev = jnp.where(first, NEG_INF, s_prev)
            logit = jnp.where(from_prev, s_prev, s_cur)
            m = jnp.maximum(jnp.max(logit, axis=-1, keepdims=True), sink)
            e = jnp.exp(logit - m)
            denom = jnp.sum(e, axis=-1, keepdims=True) + jnp.exp(sink - m)
            inv.append(1.0 / denom)
            eb = e.astype(BF16)
            tiles += [jnp.where(from_prev, eb, zero), jnp.where(from_prev, zero, eb)]
        p_rows.append(jnp.concatenate(tiles, axis=1))
    o = _mm(jnp.concatenate(p_rows, axis=0), v_bd)
    out_tiles = []
    for p in range(ATTN_PAIRS_PER_KV):
        scale = jnp.where(lane_q, inv[2 * p], inv[2 * p + 1])
        out_tiles.append((o[p * blk:(p + 1) * blk, :] * scale).astype(BF16))
    return out_tiles


def _attn_layer_body(sink_ref, x_ref, w_ref, bqkv_ref, wo_ref, bo_ref, g_ref, b_ref, o_ref, kv_ref):
    blk = ATTN_BLOCK
    tile = x_ref.shape[0]
    at_start = pl.program_id(1) == 0

    @pl.when(at_start)
    def _():
        kv_ref[...] = jnp.zeros_like(kv_ref)

    x = x_ref[...]
    xb = x.astype(BF16)
    kv = (_mm(xb, w_ref[:, ATTN_Q_DIM:].astype(BF16)) + bqkv_ref[:, ATTN_Q_DIM:]).astype(BF16)
    q = ((_mm(xb, w_ref[:, :ATTN_Q_DIM].astype(BF16)) + bqkv_ref[:, :ATTN_Q_DIM])
         * (ATTN_HEAD_DIM ** -0.5)).astype(BF16)
    units = [(r, j) for r in range(tile // blk) for j in range(ATTN_KV_HEADS)]

    def scores(u):
        r, j = units[u]
        rows = slice(r * blk, (r + 1) * blk)
        kv_prev = kv_ref[...].astype(BF16) if r == 0 else kv[(r - 1) * blk:r * blk]
        return _swa_scores(q[rows], kv[rows], kv_prev, j)

    out_tiles = {}
    pending = scores(0)
    for u, (r, j) in enumerate(units):
        s, v_bd = pending
        if u + 1 < len(units):
            pending = scores(u + 1)
        out_tiles[(r, j)] = _swa_finish(s, v_bd, sink_ref, at_start if r == 0 else None, j)
    kv_ref[...] = kv[tile - blk:].astype(F32)
    heads = jnp.concatenate(
        [jnp.concatenate([t for j in range(ATTN_KV_HEADS) for t in out_tiles[(r, j)]], axis=1)
         for r in range(tile // blk)], axis=0)
    mix = _mm(heads, wo_ref[...].astype(BF16)) + bo_ref[...]
    o_ref[...] = _layer_norm(DEEPNORM_ALPHA * x + mix, g_ref[...], b_ref[...])


def _attn_layer(x, w_qkv, b_qkv, sinks, wo, bo, layer, g, b, batch, seq):
    d = x.shape[-1]
    tile = ATTN_TILE
    x = x.reshape(batch, seq, d)
    block_bytes = 2 * 2 * tile * d * 4 + (w_qkv[0].size + wo[0].size) * 4
    temp_bytes = 8 * tile * d * 4
    out = pl.pallas_call(
        _attn_layer_body,
        grid=(batch, seq // tile),
        in_specs=[
            pl.BlockSpec(memory_space=pltpu.SMEM),
            pl.BlockSpec((None, tile, d), lambda bi, n: (bi, n, 0)),
            _layer_block(w_qkv.shape, layer),
            _resident((1, ATTN_PROJ_DIM)),
            _layer_block(wo.shape, layer),
            _resident((1, d)),
            _resident((1, d)),
            _resident((1, d)),
        ],
        out_specs=pl.BlockSpec((None, tile, d), lambda bi, n: (bi, n, 0)),
        out_shape=jax.ShapeDtypeStruct((batch, seq, d), F32),
        scratch_shapes=[pltpu.VMEM((ATTN_BLOCK, 2 * ATTN_KV_DIM), F32)],
        compiler_params=pltpu.CompilerParams(
            dimension_semantics=("parallel", "arbitrary"),
            vmem_limit_bytes=_vmem_limit(block_bytes, temp_bytes)),
        name="attention_layer",
    )(sinks, x, w_qkv, b_qkv.reshape(1, ATTN_PROJ_DIM), wo, bo.reshape(1, d), g.reshape(1, d), b.reshape(1, d))
    return out.reshape(batch * seq, d)


def _ret_proj_columns(h):
    return ((h * RET_QK_DIM, RET_QK_DIM),
            (RET_QK_ALL + h * RET_QK_DIM, RET_QK_DIM),
            (2 * RET_QK_ALL + h * RET_V_DIM, RET_V_DIM),
            (2 * RET_QK_ALL + RET_V_ALL + h * RET_V_DIM, RET_V_DIM))


def _ret_layer_body(tiles_per_seq, xc_ref, xn_ref, w_ref, wo_ref, cos_ref, sin_ref, g_ref, b_ref, o_ref,
                    state_ref, decay_ref, *proj_refs):
    c = RET_STEP
    n_chunks = xc_ref.shape[0] // c
    step = pl.program_id(0)

    def project(h, xb, parts=(0, 1, 2, 3)):
        columns = _ret_proj_columns(h)
        return [_mm(xb, w_ref[:, columns[i][0]:columns[i][0] + columns[i][1]].astype(BF16)) for i in parts]

    def park_head0(xb, parts):
        for i, value in zip(parts, project(0, xb, parts)):
            proj_refs[i][...] = value

    @pl.when(step == 0)
    def _():
        state_ref[...] = jnp.zeros_like(state_ref)
        row = lax.broadcasted_iota(jnp.int32, (c, c), 0).astype(F32)
        col = lax.broadcasted_iota(jnp.int32, (c, c), 1).astype(F32)
        diff = row - col
        for h in range(RET_HEADS):
            decay_ref[h] = jnp.where(diff >= 0, jnp.exp(jnp.maximum(diff, 0.0) * RET_LOG_GAMMA[h]), 0.0)
        park_head0(xn_ref[...].astype(BF16), (0, 1, 2, 3))

    def load_head(h, proj):
        lg = RET_LOG_GAMMA[h]
        idx = lax.broadcasted_iota(jnp.int32, (c, 1), 0).astype(F32)
        q_decay = jnp.exp((idx + 1.0) * lg)
        k_decay = jnp.exp((c - 1.0 - idx) * lg)
        out = []
        for r in range(n_chunks):
            rows = slice(r * c, (r + 1) * c)
            cos = cos_ref[rows, :]
            sin = sin_ref[rows, :]

            def rot(t):
                t1 = t[:, :ROPE_HALF]
                t2 = t[:, ROPE_HALF:]
                return jnp.concatenate([t1 * cos - t2 * sin, t1 * sin + t2 * cos], axis=-1)

            q = rot(proj[0][rows, :])
            k = rot(proj[1][rows, :]) * (RET_QK_DIM ** -0.5)
            v = proj[2][rows, :].astype(BF16)
            gate = proj[3][rows, :]
            out.append((q.astype(BF16), (q * q_decay).astype(BF16), k.astype(BF16), (k * k_decay).astype(BF16), v,
                        gate * jax.nn.sigmoid(gate)))
        return out

    def core(h, chunks, fresh):
        chunk_decay = math.exp(c * RET_LOG_GAMMA[h])
        decay_in = decay_ref[h]
        gated = []
        for r, (qb, qd, kb, kd, v, sg) in enumerate(chunks):
            qk = _mm_nt(qb, kb) * decay_in
            state = state_ref[h]
            if r == 0:
                state = jnp.where(fresh, 0.0, state)
            cross = _mm(qd, state.astype(BF16))
            kv = _mm_tn(kd, v)
            inner = _mm(qk.astype(BF16), v)
            state_ref[h] = state * chunk_decay + kv
            y = inner + cross
            mu = jnp.mean(y, axis=-1, keepdims=True)
            yc = y - mu
            var = jnp.mean(yc * yc, axis=-1, keepdims=True)
            gated.append((sg * (yc * lax.rsqrt(var + GN_EPS))).astype(BF16))
        return jnp.concatenate(gated, axis=0)

    @pl.when(step > 0)
    def _():
        fresh = lax.rem(step - 1, tiles_per_seq) == 0
        x = xc_ref[...]
        xb = x.astype(BF16)
        xb_next = xn_ref[...].astype(BF16)
        loaded = load_head(0, proj_refs)
        upcoming = project(1, xb)
        mix = None
        for h in range(RET_HEADS):
            gated = core(h, loaded, fresh)
            if h + 1 < RET_HEADS:
                loaded = load_head(h + 1, upcoming)
            if h + 2 < RET_HEADS:
                upcoming = project(h + 2, xb)
            elif h + 2 == RET_HEADS:
                park_head0(xb_next, (0, 1))
            else:
                park_head0(xb_next, (2,))
            part = _mm(gated, wo_ref[h * RET_V_DIM:(h + 1) * RET_V_DIM, :])
            mix = part if mix is None else mix + part
        park_head0(xb_next, (3,))
        o_ref[...] = _layer_norm(DEEPNORM_ALPHA * x + mix, g_ref[...], b_ref[...])


def _ret_layer(x, w, wo, layer, cos, sin, g, b, seq):
    t, d = x.shape
    tile = TOKEN_TILE
    n_tiles = t // tile
    tiles_per_seq = seq // tile
    proj_shapes = [pltpu.VMEM((tile, width), F32) for _, width in _ret_proj_columns(0)]
    head_bytes = tile * (RET_PROJ_DIM // RET_HEADS) * 4
    block_bytes = (3 * 2 * tile * d * 4 + w[0].size * 4 + wo[0].size * 2 + 2 * 2 * tile * ROPE_HALF * 4
                   + head_bytes)
    temp_bytes = (RET_HEADS * RET_QK_DIM * RET_V_DIM * 4 + RET_HEADS * RET_STEP * RET_STEP * 4
                  + 3 * head_bytes)
    prev_tile = lambda s: jnp.maximum(s - 1, 0)
    return pl.pallas_call(
        functools.partial(_ret_layer_body, tiles_per_seq),
        grid=(n_tiles + 1,),
        in_specs=[
            pl.BlockSpec((tile, d), lambda s: (prev_tile(s), 0)),
            pl.BlockSpec((tile, d), lambda s: (jnp.minimum(s, n_tiles - 1), 0)),
            _layer_block(w.shape, layer),
            _layer_block(wo.shape, layer),
            pl.BlockSpec((tile, ROPE_HALF), lambda s: (lax.rem(prev_tile(s), tiles_per_seq), 0)),
            pl.BlockSpec((tile, ROPE_HALF), lambda s: (lax.rem(prev_tile(s), tiles_per_seq), 0)),
            _resident((1, d)),
            _resident((1, d)),
        ],
        out_specs=pl.BlockSpec((tile, d), lambda s: (prev_tile(s), 0)),
        out_shape=jax.ShapeDtypeStruct((t, d), F32),
        scratch_shapes=[pltpu.VMEM((RET_HEADS, RET_QK_DIM, RET_V_DIM), F32),
                        pltpu.VMEM((RET_HEADS, RET_STEP, RET_STEP), F32)] + proj_shapes,
        compiler_params=pltpu.CompilerParams(
            dimension_semantics=("arbitrary",),
            vmem_limit_bytes=_vmem_limit(block_bytes, temp_bytes)),
        name="retention_layer",
    )(x, x, w, wo, cos, sin, g.reshape(1, d), b.reshape(1, d))


def _rope_tables(seq):
    pos = jnp.arange(seq, dtype=F32)
    freqs = 1.0 / (ROPE_BASE ** jnp.linspace(0.0, 1.0, ROPE_HALF, dtype=F32))
    ang = pos[:, None] * freqs[None, :]
    return jnp.cos(ang), jnp.sin(ang)


def kernel(x, ln_g, ln_b, ffn1_w_gate_up, ffn1_w_down, ffn2_w_gate_up, ffn2_w_down,
           attn_w_qkv, attn_b_qkv, attn_sinks, attn_w_o, attn_b_o, ret_w_qkvg, ret_w_o):
    batch, seq, d = x.shape
    assert d == D_MODEL and seq % TOKEN_TILE == 0 and TOKEN_TILE % RET_STEP == 0
    assert seq % ATTN_TILE == 0 and ATTN_TILE % ATTN_BLOCK == 0
    cos, sin = _rope_tables(seq)
    ret_wo = ret_w_o.astype(BF16)
    h = x.reshape(batch * seq, d)
    for i in range(DEPTH):
        h = _ffn_ln(h, ffn1_w_gate_up, ffn1_w_down, i, ln_g[i, 0], ln_b[i, 0])
        j = i // N_MIXERS
        if i % N_MIXERS == 0:
            h = _attn_layer(h, attn_w_qkv, attn_b_qkv[j], attn_sinks[j], attn_w_o, attn_b_o[j], j,
                            ln_g[i, 1], ln_b[i, 1], batch, seq)
        else:
            h = _ret_layer(h, ret_w_qkvg, ret_wo, j, cos, sin, ln_g[i, 1], ln_b[i, 1], seq)
        h = _ffn_ln(h, ffn2_w_gate_up, ffn2_w_down, i, ln_g[i, 2], ln_b[i, 2])
    return h.reshape(batch, seq, d)
```

```python
import functools
import math

import jax
import jax.numpy as jnp
from jax import lax
from jax.experimental import pallas as pl
from jax.experimental.pallas import tpu as pltpu

D_MODEL = 1024
DEPTH = 4
N_MIXERS = 2
ATTN_Q_HEADS = 16
ATTN_KV_HEADS = 2
ATTN_HEAD_DIM = 64
WINDOW = 128
ATTN_BLOCK = 128
RET_HEADS = 4
RET_QK_DIM = D_MODEL // RET_HEADS
RET_V_DIM = 2 * D_MODEL // RET_HEADS
ROPE_BASE = 10000.0
FFN_DIM = 2816
DEEPNORM_ALPHA = (2.0 * DEPTH) ** 0.25
LN_EPS = 1e-5
GN_EPS = 1e-6
NEG_INF = -1e30

ATTN_GROUP = ATTN_Q_HEADS // ATTN_KV_HEADS
ATTN_Q_DIM = ATTN_Q_HEADS * ATTN_HEAD_DIM
ATTN_KV_DIM = ATTN_KV_HEADS * ATTN_HEAD_DIM
RET_QK_ALL = RET_HEADS * RET_QK_DIM
RET_V_ALL = RET_HEADS * RET_V_DIM
RET_PROJ_DIM = 2 * RET_QK_ALL + 2 * RET_V_ALL
ROPE_HALF = RET_QK_DIM // 2
RET_LOG_GAMMA = tuple(math.log(1.0 - 2.0 ** (-5.0 - h)) for h in range(RET_HEADS))

V7X_LANES = 128
V7X_MXU_DIM = 256
V7X_VMEM_BYTES = 64 * 1024 * 1024
V7X_VMEM_SCOPED_MAX = 56 * 1024 * 1024

TOKEN_TILE = 512
ATTN_TILE = 1024
FFN_CHUNK = V7X_MXU_DIM
RET_STEP = V7X_MXU_DIM

F32 = jnp.float32
BF16 = jnp.bfloat16


def _vmem_limit(block_bytes, temp_bytes):
    return int(min(block_bytes + temp_bytes, V7X_VMEM_SCOPED_MAX))


def _resident(shape):
    return pl.BlockSpec(shape, lambda *_: (0,) * len(shape), pipeline_mode=pl.Buffered(1))


def _layer_block(shape, layer):
    return pl.BlockSpec((None,) + tuple(shape[1:]), lambda *_: (layer,) + (0,) * (len(shape) - 1),
                        pipeline_mode=pl.Buffered(1))


def _layer_norm(y, g, b):
    mu = jnp.mean(y, axis=-1, keepdims=True)
    yc = y - mu
    var = jnp.mean(yc * yc, axis=-1, keepdims=True)
    return yc * lax.rsqrt(var + LN_EPS) * g + b


def _mm(a, b):
    return jnp.dot(a, b, preferred_element_type=F32)


def _mm_nt(a, b):
    return lax.dot_general(a, b, (((1,), (1,)), ((), ())), preferred_element_type=F32)


def _mm_tn(a, b):
    return lax.dot_general(a, b, (((0,), (0,)), ((), ())), preferred_element_type=F32)


FFN_N_CHUNKS = FFN_DIM // FFN_CHUNK


def _ffn_weight_copies(layer, c, wgu_hbm, wd_hbm, wgu_ref, wd_ref, sem):
    lo = c * FFN_CHUNK
    return (
        pltpu.make_async_copy(wgu_hbm.at[layer, :, pl.ds(lo, FFN_CHUNK)],
                              wgu_ref.at[:, pl.ds(lo, FFN_CHUNK)], sem.at[0, c]),
        pltpu.make_async_copy(wgu_hbm.at[layer, :, pl.ds(FFN_DIM + lo, FFN_CHUNK)],
                              wgu_ref.at[:, pl.ds(FFN_DIM + lo, FFN_CHUNK)], sem.at[1, c]),
        pltpu.make_async_copy(wd_hbm.at[layer, pl.ds(lo, FFN_CHUNK), :],
                              wd_ref.at[pl.ds(lo, FFN_CHUNK), :], sem.at[2, c]),
    )


def _ffn_ln_body(layer, x_ref, wgu_hbm, wd_hbm, g_ref, b_ref, o_ref, wgu_ref, wd_ref, sem):
    step = pl.program_id(0)

    def copies(c):
        return _ffn_weight_copies(layer, c, wgu_hbm, wd_hbm, wgu_ref, wd_ref, sem)

    def tile(wait_for_weights):
        x = x_ref[...]
        xb = x.astype(BF16)
        hidden = []
        for c in range(FFN_N_CHUNKS):
            lo = c * FFN_CHUNK
            if wait_for_weights:
                for cp in copies(c):
                    cp.wait()
            gate = _mm(xb, wgu_ref[:, lo:lo + FFN_CHUNK].astype(BF16))
            up = _mm(xb, wgu_ref[:, FFN_DIM + lo:FFN_DIM + lo + FFN_CHUNK].astype(BF16))
            hidden.append((gate * jax.nn.sigmoid(gate) * up).astype(BF16))
        acc = _mm(jnp.concatenate(hidden, axis=1), wd_ref[...].astype(BF16))
        y = DEEPNORM_ALPHA * x + 0.5 * acc
        o_ref[...] = _layer_norm(y, g_ref[...], b_ref[...])

    @pl.when(step == 0)
    def _():
        for c in range(FFN_N_CHUNKS):
            for cp in copies(c):
                cp.start()
        tile(wait_for_weights=True)

    @pl.when(step > 0)
    def _():
        tile(wait_for_weights=False)


def _ffn_ln(x, w_gate_up, w_down, layer, g, b):
    t, d = x.shape
    tm = TOKEN_TILE
    block_bytes = 2 * 2 * tm * d * 4 + (w_gate_up[0].size + w_down[0].size) * 4
    temp_bytes = 6 * tm * d * 4
    return pl.pallas_call(
        functools.partial(_ffn_ln_body, layer),
        grid=(t // tm,),
        in_specs=[
            pl.BlockSpec((tm, d), lambda i: (i, 0)),
            pl.BlockSpec(memory_space=pl.ANY),
            pl.BlockSpec(memory_space=pl.ANY),
            _resident((1, d)),
            _resident((1, d)),
        ],
        out_specs=pl.BlockSpec((tm, d), lambda i: (i, 0)),
        out_shape=jax.ShapeDtypeStruct((t, d), F32),
        scratch_shapes=[pltpu.VMEM(w_gate_up.shape[1:], F32), pltpu.VMEM(w_down.shape[1:], F32),
                        pltpu.SemaphoreType.DMA((3, FFN_N_CHUNKS))],
        compiler_params=pltpu.CompilerParams(
            dimension_semantics=("arbitrary",),
            vmem_limit_bytes=_vmem_limit(block_bytes, temp_bytes)),
        name="ffn_ln",
    )(x, w_gate_up, w_down, g.reshape(1, d), b.reshape(1, d))


ATTN_PAIR_LANES = 2 * ATTN_HEAD_DIM
ATTN_PAIRS_PER_KV = ATTN_GROUP // 2
ATTN_PROJ_DIM = ATTN_Q_DIM + 2 * ATTN_KV_DIM
assert WINDOW == ATTN_BLOCK and ATTN_PAIR_LANES == V7X_LANES and ATTN_GROUP % 2 == 0 and ATTN_KV_HEADS == 2


def _swa_scores(q, kv_cur, kv_prev, j):
    blk = ATTN_BLOCK
    low = lax.broadcasted_iota(jnp.int32, (2 * blk, ATTN_PAIR_LANES), 1) < ATTN_HEAD_DIM
    zero = jnp.zeros((), BF16)

    def block_diagonal(t):
        swapped = pltpu.roll(t, ATTN_HEAD_DIM, 1)
        in_low, in_high = (t, swapped) if j == 0 else (swapped, t)
        return jnp.concatenate([jnp.where(low, in_low, zero), jnp.where(low, zero, in_high)], axis=0)

    k_bd = block_diagonal(jnp.concatenate([kv_prev[:, :ATTN_KV_DIM], kv_cur[:, :ATTN_KV_DIM]], axis=0))
    v_bd = block_diagonal(jnp.concatenate([kv_prev[:, ATTN_KV_DIM:], kv_cur[:, ATTN_KV_DIM:]], axis=0))
    qo = j * ATTN_GROUP * ATTN_HEAD_DIM
    q_st = jnp.concatenate(
        [q[:, qo + p * ATTN_PAIR_LANES:qo + (p + 1) * ATTN_PAIR_LANES] for p in range(ATTN_PAIRS_PER_KV)],
        axis=0)
    return _mm_nt(q_st, k_bd), v_bd


def _swa_finish(s, v_bd, sink_ref, first, j):
    blk = ATTN_BLOCK
    qi = lax.broadcasted_iota(jnp.int32, (blk, blk), 0)
    kc = lax.broadcasted_iota(jnp.int32, (blk, blk), 1)
    from_prev = kc > qi
    lane_q = lax.broadcasted_iota(jnp.int32, (blk, ATTN_PAIR_LANES), 1) < ATTN_HEAD_DIM
    zero = jnp.zeros((), BF16)
    p_rows = []
    inv = []
    for p in range(ATTN_PAIRS_PER_KV):
        tiles = []
        for par in range(2):
            sink = sink_ref[j * ATTN_GROUP + 2 * p + par]
            base = par * 2 * blk
            s_prev = s[p * blk:(p + 1) * blk, base:base + blk]
            s_cur = s[p * blk:(p + 1) * blk, base + blk:base + 2 * blk]
            if first is not None:
                s_prev = jnp.where(first, NEG_INF, s_prev)
            logit = jnp.where(from_prev, s_prev, s_cur)
            m = jnp.maximum(jnp.max(logit, axis=-1, keepdims=True), sink)
            e = jnp.exp(logit - m)
            denom = jnp.sum(e, axis=-1, keepdims=True) + jnp.exp(sink - m)
            inv.append(1.0 / denom)
            eb = e.astype(BF16)
            tiles += [jnp.where(from_prev, eb, zero), jnp.where(from_prev, zero, eb)]
        p_rows.append(jnp.concatenate(tiles, axis=1))
    o = _mm(jnp.concatenate(p_rows, axis=0), v_bd)
    out_tiles = []
    for p in range(ATTN_PAIRS_PER_KV):
        scale = jnp.where(lane_q, inv[2 * p], inv[2 * p + 1])
        out_tiles.append((o[p * blk:(p + 1) * blk, :] * scale).astype(BF16))
    return out_tiles


def _attn_layer_body(sink_ref, x_ref, w_ref, bqkv_ref, wo_ref, bo_ref, g_ref, b_ref, o_ref, kv_ref):
    blk = ATTN_BLOCK
    tile = x_ref.shape[0]
    at_start = pl.program_id(1) == 0

    @pl.when(at_start)
    def _():
        kv_ref[...] = jnp.zeros_like(kv_ref)

    x = x_ref[...]
    xb = x.astype(BF16)
    kv = (_mm(xb, w_ref[:, ATTN_Q_DIM:].astype(BF16)) + bqkv_ref[:, ATTN_Q_DIM:]).astype(BF16)
    q = ((_mm(xb, w_ref[:, :ATTN_Q_DIM].astype(BF16)) + bqkv_ref[:, :ATTN_Q_DIM])
         * (ATTN_HEAD_DIM ** -0.5)).astype(BF16)
    units = [(r, j) for r in range(tile // blk) for j in range(ATTN_KV_HEADS)]

    def scores(u):
        r, j = units[u]
        rows = slice(r * blk, (r + 1) * blk)
        kv_prev = kv_ref[...].astype(BF16) if r == 0 else kv[(r - 1) * blk:r * blk]
        return _swa_scores(q[rows], kv[rows], kv_prev, j)

    out_tiles = {}
    pending = scores(0)
    for u, (r, j) in enumerate(units):
        s, v_bd = pending
        if u + 1 < len(units):
            pending = scores(u + 1)
        out_tiles[(r, j)] = _swa_finish(s, v_bd, sink_ref, at_start if r == 0 else None, j)
    kv_ref[...] = kv[tile - blk:].astype(F32)
    heads = jnp.concatenate(
        [jnp.concatenate([t for j in range(ATTN_KV_HEADS) for t in out_tiles[(r, j)]], axis=1)
         for r in range(tile // blk)], axis=0)
    mix = _mm(heads, wo_ref[...].astype(BF16)) + bo_ref[...]
    o_ref[...] = _layer_norm(DEEPNORM_ALPHA * x + mix, g_ref[...], b_ref[...])


def _attn_layer(x, w_qkv, b_qkv, sinks, wo, bo, layer, g, b, batch, seq):
    d = x.shape[-1]
    tile = ATTN_TILE
    x = x.reshape(batch, seq, d)
    block_bytes = 2 * 2 * tile * d * 4 + (w_qkv[0].size + wo[0].size) * 4
    temp_bytes = 8 * tile * d * 4
    out = pl.pallas_call(
        _attn_layer_body,
        grid=(batch, seq // tile),
        in_specs=[
            pl.BlockSpec(memory_space=pltpu.SMEM),
            pl.BlockSpec((None, tile, d), lambda bi, n: (bi, n, 0)),
            _layer_block(w_qkv.shape, layer),
            _resident((1, ATTN_PROJ_DIM)),
            _layer_block(wo.shape, layer),
            _resident((1, d)),
            _resident((1, d)),
            _resident((1, d)),
        ],
        out_specs=pl.BlockSpec((None, tile, d), lambda bi, n: (bi, n, 0)),
        out_shape=jax.ShapeDtypeStruct((batch, seq, d), F32),
        scratch_shapes=[pltpu.VMEM((ATTN_BLOCK, 2 * ATTN_KV_DIM), F32)],
        compiler_params=pltpu.CompilerParams(
            dimension_semantics=("parallel", "arbitrary"),
            vmem_limit_bytes=_vmem_limit(block_bytes, temp_bytes)),
        name="attention_layer",
    )(sinks, x, w_qkv, b_qkv.reshape(1, ATTN_PROJ_DIM), wo, bo.reshape(1, d), g.reshape(1, d), b.reshape(1, d))
    return out.reshape(batch * seq, d)


def _ret_proj_columns(h):
    return ((h * RET_QK_DIM, RET_QK_DIM),
            (RET_QK_ALL + h * RET_QK_DIM, RET_QK_DIM),
            (2 * RET_QK_ALL + h * RET_V_DIM, RET_V_DIM),
            (2 * RET_QK_ALL + RET_V_ALL + h * RET_V_DIM, RET_V_DIM))


def _ret_layer_body(tiles_per_seq, xc_ref, xn_ref, w_ref, wo_ref, cos_ref, sin_ref, g_ref, b_ref, o_ref,
                    state_ref, decay_ref, *proj_refs):
    c = RET_STEP
    n_chunks = xc_ref.shape[0] // c
    step = pl.program_id(0)

    def project(h, xb, parts=(0, 1, 2, 3)):
        columns = _ret_proj_columns(h)
        return [_mm(xb, w_ref[:, columns[i][0]:columns[i][0] + columns[i][1]].astype(BF16)) for i in parts]

    def park_head0(xb, parts):
        for i, value in zip(parts, project(0, xb, parts)):
            proj_refs[i][...] = value

    @pl.when(step == 0)
    def _():
        state_ref[...] = jnp.zeros_like(state_ref)
        row = lax.broadcasted_iota(jnp.int32, (c, c), 0).astype(F32)
        col = lax.broadcasted_iota(jnp.int32, (c, c), 1).astype(F32)
        diff = row - col
        for h in range(RET_HEADS):
            decay_ref[h] = jnp.where(diff >= 0, jnp.exp(jnp.maximum(diff, 0.0) * RET_LOG_GAMMA[h]), 0.0)
        park_head0(xn_ref[...].astype(BF16), (0, 1, 2, 3))

    def load_head(h, proj):
        lg = RET_LOG_GAMMA[h]
        idx = lax.broadcasted_iota(jnp.int32, (c, 1), 0).astype(F32)
        q_decay = jnp.exp((idx + 1.0) * lg)
        k_decay = jnp.exp((c - 1.0 - idx) * lg)
        out = []
        for r in range(n_chunks):
            rows = slice(r * c, (r + 1) * c)
            cos = cos_ref[rows, :]
            sin = sin_ref[rows, :]

            def rot(t):
                t1 = t[:, :ROPE_HALF]
                t2 = t[:, ROPE_HALF:]
                return jnp.concatenate([t1 * cos - t2 * sin, t1 * sin + t2 * cos], axis=-1)

            q = rot(proj[0][rows, :])
            k = rot(proj[1][rows, :]) * (RET_QK_DIM ** -0.5)
            v = proj[2][rows, :].astype(BF16)
            gate = proj[3][rows, :]
            out.append((q.astype(BF16), (q * q_decay).astype(BF16), k.astype(BF16), (k * k_decay).astype(BF16), v,
                        gate * jax.nn.sigmoid(gate)))
        return out

    def core(h, chunks, fresh):
        chunk_decay = math.exp(c * RET_LOG_GAMMA[h])
        decay_in = decay_ref[h]
        gated = []
        for r, (qb, qd, kb, kd, v, sg) in enumerate(chunks):
            qk = _mm_nt(qb, kb) * decay_in
            state = state_ref[h]
            if r == 0:
                state = jnp.where(fresh, 0.0, state)
            cross = _mm(qd, state.astype(BF16))
            kv = _mm_tn(kd, v)
            inner = _mm(qk.astype(BF16), v)
            state_ref[h] = state * chunk_decay + kv
            y = inner + cross
            mu = jnp.mean(y, axis=-1, keepdims=True)
            yc = y - mu
            var = jnp.mean(yc * yc, axis=-1, keepdims=True)
            gated.append((sg * (yc * lax.rsqrt(var + GN_EPS))).astype(BF16))
        return jnp.concatenate(gated, axis=0)

    @pl.when(step > 0)
    def _():
        fresh = lax.rem(step - 1, tiles_per_seq) == 0
        x = xc_ref[...]
        xb = x.astype(BF16)
        xb_next = xn_ref[...].astype(BF16)
        loaded = load_head(0, proj_refs)
        upcoming = project(1, xb)
        mix = None
        for h in range(RET_HEADS):
            gated = core(h, loaded, fresh)
            if h + 1 < RET_HEADS:
                loaded = load_head(h + 1, upcoming)
            if h + 2 < RET_HEADS:
                upcoming = project(h + 2, xb)
            elif h + 2 == RET_HEADS:
                park_head0(xb_next, (0, 1))
            else:
                park_head0(xb_next, (2,))
            part = _mm(gated, wo_ref[h * RET_V_DIM:(h + 1) * RET_V_DIM, :])
            mix = part if mix is None else mix + part
        park_head0(xb_next, (3,))
        o_ref[...] = _layer_norm(DEEPNORM_ALPHA * x + mix, g_ref[...], b_ref[...])


def _ret_layer(x, w, wo, layer, cos, sin, g, b, seq):
    t, d = x.shape
    tile = TOKEN_TILE
    n_tiles = t // tile
    tiles_per_seq = seq // tile
    proj_shapes = [pltpu.VMEM((tile, width), F32) for _, width in _ret_proj_columns(0)]
    head_bytes = tile * (RET_PROJ_DIM // RET_HEADS) * 4
    block_bytes = (3 * 2 * tile * d * 4 + w[0].size * 4 + wo[0].size * 2 + 2 * 2 * tile * ROPE_HALF * 4
                   + head_bytes)
    temp_bytes = (RET_HEADS * RET_QK_DIM * RET_V_DIM * 4 + RET_HEADS * RET_STEP * RET_STEP * 4
                  + 3 * head_bytes)
    prev_tile = lambda s: jnp.maximum(s - 1, 0)
    return pl.pallas_call(
        functools.partial(_ret_layer_body, tiles_per_seq),
        grid=(n_tiles + 1,),
        in_specs=[
            pl.BlockSpec((tile, d), lambda s: (prev_tile(s), 0)),
            pl.BlockSpec((tile, d), lambda s: (jnp.minimum(s, n_tiles - 1), 0)),
            _layer_block(w.shape, layer),
            _layer_block(wo.shape, layer),
            pl.BlockSpec((tile, ROPE_HALF), lambda s: (lax.rem(prev_tile(s), tiles_per_seq), 0)),
            pl.BlockSpec((tile, ROPE_HALF), lambda s: (lax.rem(prev_tile(s), tiles_per_seq), 0)),
            _resident((1, d)),
            _resident((1, d)),
        ],
        out_specs=pl.BlockSpec((tile, d), lambda s: (prev_tile(s), 0)),
        out_shape=jax.ShapeDtypeStruct((t, d), F32),
        scratch_shapes=[pltpu.VMEM((RET_HEADS, RET_QK_DIM, RET_V_DIM), F32),
                        pltpu.VMEM((RET_HEADS, RET_STEP, RET_STEP), F32)] + proj_shapes,
        compiler_params=pltpu.CompilerParams(
            dimension_semantics=("arbitrary",),
            vmem_limit_bytes=_vmem_limit(block_bytes, temp_bytes)),
        name="retention_layer",
    )(x, x, w, wo, cos, sin, g.reshape(1, d), b.reshape(1, d))


def _rope_tables(seq):
    pos = jnp.arange(seq, dtype=F32)
    freqs = 1.0 / (ROPE_BASE ** jnp.linspace(0.0, 1.0, ROPE_HALF, dtype=F32))
    ang = pos[:, None] * freqs[None, :]
    return jnp.cos(ang), jnp.sin(ang)


def kernel(x, ln_g, ln_b, ffn1_w_gate_up, ffn1_w_down, ffn2_w_gate_up, ffn2_w_down,
           attn_w_qkv, attn_b_qkv, attn_sinks, attn_w_o, attn_b_o, ret_w_qkvg, ret_w_o):
    batch, seq, d = x.shape
    assert d == D_MODEL and seq % TOKEN_TILE == 0 and TOKEN_TILE % RET_STEP == 0
    assert seq % ATTN_TILE == 0 and ATTN_TILE % ATTN_BLOCK == 0
    cos, sin = _rope_tables(seq)
    ret_wo = ret_w_o.astype(BF16)
    h = x.reshape(batch * seq, d)
    for i in range(DEPTH):
        h = _ffn_ln(h, ffn1_w_gate_up, ffn1_w_down, i, ln_g[i, 0], ln_b[i, 0])
        j = i // N_MIXERS
        if i % N_MIXERS == 0:
            h = _attn_layer(h, attn_w_qkv, attn_b_qkv[j], attn_sinks[j], attn_w_o, attn_b_o[j], j,
                            ln_g[i, 1], ln_b[i, 1], batch, seq)
        else:
            h = _ret_layer(h, ret_w_qkvg, ret_wo, j, cos, sin, ln_g[i, 1], ln_b[i, 1], seq)
        h = _ffn_ln(h, ffn2_w_gate_up, ffn2_w_down, i, ln_g[i, 2], ln_b[i, 2])
    return h.reshape(batch, seq, d)
```
